```python
import math
import jax, jax.numpy as jnp
from jax import lax
import numpy as np

D_MODEL = 1024
BATCH = 2
SEQ = 16384
DEPTH = 2

CTX_LEN = 256
GRID_W = 64
GROUP_W = D_MODEL // 4
D_MIX = 4 * GROUP_W
HEAD_DIM = 64
EPS = 1e-6
A_BLOCKS = 4
A_BLK = GROUP_W // A_BLOCKS
CONV_W = 4
LRU_C = 8.0
B_HEADS = 4
B_DK = GROUP_W // B_HEADS
B_DV = GROUP_W // B_HEADS
B_CHUNK = 128
C_HEADS = 4
C_KV = 2
WINDOW = 128
BLK = 128
D_HEADS = 4
D_KV = 2
ROPE_THETA = 10000.0
N_EXPERTS = 16
CAP_FACTOR = 2
D_EXPERT = 1024

IN_WIDTHS = (GROUP_W, GROUP_W,
             GROUP_W, GROUP_W, GROUP_W, GROUP_W, GROUP_W,
             C_HEADS * HEAD_DIM, C_KV * HEAD_DIM, C_KV * HEAD_DIM,
             D_HEADS * HEAD_DIM, D_KV * HEAD_DIM, D_KV * HEAD_DIM)
D_IN = sum(IN_WIDTHS)

kernel_name = "hybrid_parallel_heads_ec_moe_dit"


def rmsnorm(x, w):
    x32 = x.astype(jnp.float32)
    y = x32 * lax.rsqrt(jnp.mean(x32 * x32, axis=-1, keepdims=True) + EPS)
    return y.astype(x.dtype) * w


def modulate(h, shift, scale):
    return h * (1.0 + scale) + shift


def split_in(z):
    offs = np.cumsum(IN_WIDTHS)[:-1].tolist()
    return jnp.split(z, offs, axis=-1)


def split_heads(t, n):
    return t.reshape(*t.shape[:-1], n, t.shape[-1] // n)


def axial_rope(rows):
    half = HEAD_DIM // 2
    n_freq = half // 2
    inv = ROPE_THETA ** (-jnp.arange(n_freq, dtype=jnp.float32) * 2.0 / half)
    r = jnp.repeat(jnp.arange(rows, dtype=jnp.float32), GRID_W)
    col = jnp.tile(jnp.arange(GRID_W, dtype=jnp.float32), rows)
    ang = jnp.stack([r[:, None] * inv, col[:, None] * inv], axis=1)
    return jnp.cos(ang), jnp.sin(ang)


def apply_rope(x, cos, sin):
    xr = x.reshape(*x.shape[:-1], 2, 2, HEAD_DIM // 4)
    x1, x2 = xr[..., 0, :], xr[..., 1, :]
    c = cos[None, :, None].astype(x.dtype)
    s = sin[None, :, None].astype(x.dtype)
    return jnp.stack([x1 * c - x2 * s, x2 * c + x1 * s], axis=-2).reshape(x.shape)


def conv_centred(x, w, b):
    left = CONV_W // 2
    xp = jnp.pad(x, ((0, 0), (left, CONV_W - 1 - left), (0, 0)))
    L = x.shape[1]
    return sum(xp[:, j:j + L] * w[j] for j in range(CONV_W)) + b


def block_diag(x, w, b):
    xb = x.reshape(*x.shape[:-1], A_BLOCKS, A_BLK)
    return jnp.einsum('blni,nij->blnj', xb, w).reshape(x.shape) + b


def lru_coeffs(x, wa, ba, wx, bx, lam):
    xf = x.astype(jnp.float32)
    r = jax.nn.sigmoid(block_diag(xf, wa, ba))
    i = jax.nn.sigmoid(block_diag(xf, wx, bx))
    log_a = -LRU_C * r * jax.nn.softplus(-lam.astype(jnp.float32))
    a = jnp.exp(log_a)
    u = jnp.sqrt(-jnp.expm1(2.0 * log_a)) * (i * xf)
    return a, u


def _lin_combine(e1, e2):
    a1, u1 = e1
    a2, u2 = e2
    return a1 * a2, a2 * u1 + u2


def linear_scan(a, u, h0):
    u = u.at[:, 0].add(a[:, 0] * h0)
    _, h = lax.associative_scan(_lin_combine, (a, u), axis=1)
    return h


def hgrn_forget(f_raw, lb):
    z = f_raw.astype(jnp.float32)
    log_f = jnp.logaddexp(jnp.log(lb), jnp.log1p(-lb) + jax.nn.log_sigmoid(z))
    k = (1.0 - lb) * jax.nn.sigmoid(-z)
    return log_f, k


def hgrn_inputs(q_raw, f_fwd, f_bwd, i_raw, lb):
    q = split_heads(jax.nn.silu(q_raw.astype(jnp.float32)) * (B_DK ** -0.5), B_HEADS)
    v = split_heads(i_raw.astype(jnp.float32), B_HEADS)
    dirs = []
    for f_raw in (f_fwd, f_bwd):
        log_f, k = hgrn_forget(f_raw, lb)
        dirs.append((q, split_heads(k, B_HEADS), v, split_heads(log_f, B_HEADS)))
    return dirs


def hgrn2_chunk_scan(q, k, v, logf, S0):
    Bn, L, H, dk = q.shape
    n = L // B_CHUNK
    mask = jnp.tril(jnp.ones((B_CHUNK, B_CHUNK), dtype=bool))[None, :, :, None, None]

    def to_chunks(t):
        return jnp.moveaxis(t.reshape(Bn, n, B_CHUNK, *t.shape[2:]), 1, 0)

    def step(S, inp):
        qc, kc, vc, gc = inp
        b = jnp.cumsum(gc, axis=1)
        o = jnp.einsum('bthk,bhkv->bthv', qc * jnp.exp(b), S)
        decay = jnp.exp(jnp.where(mask, b[:, :, None] - b[:, None, :], -jnp.inf))
        attn = jnp.einsum('bthk,bshk,btshk->bhts', qc, kc, decay)
        o = o + jnp.einsum('bhts,bshv->bthv', attn, vc)
        b_last = b[:, -1]
        S = jnp.exp(b_last)[..., None] * S + jnp.einsum('bshk,bshv->bhkv', kc * jnp.exp(b_last[:, None] - b), vc)
        return S, o

    S, o = lax.scan(step, S0, tuple(to_chunks(t) for t in (q, k, v, logf)))
    return jnp.moveaxis(o, 0, 1).reshape(Bn, L, H, v.shape[-1]), S


def hgrn_readout(o, g, w):
    y = rmsnorm(o, w)
    return y.reshape(*y.shape[:-2], GROUP_W).astype(g.dtype) * jax.nn.silu(g)


def bidirectional(scan_dir, ctx_ins, lat_ins, ctx_out):
    y_ctx, y_lat = None, None
    for d in range(2):
        flip = (lambda t: jnp.flip(t, axis=1)) if d == 1 else (lambda t: t)
        yc, state = scan_dir(d, [flip(t) for t in ctx_ins[d]], None)
        yl, _ = scan_dir(d, [flip(t) for t in lat_ins[d]], state)
        y_lat = flip(yl) if y_lat is None else y_lat + flip(yl)
        if ctx_out:
            y_ctx = flip(yc) if y_ctx is None else y_ctx + flip(yc)
    return y_ctx, y_lat


def ctx_attention(q, k, v, sink):
    Bn, Lc, Hq, d = q.shape
    hk = k.shape[2]
    g = Hq // hk
    qg = q.reshape(Bn, Lc, hk, g, d) * (d ** -0.5)
    s = jnp.einsum('bqhgd,bkhd->bhgqk', qg, k).astype(jnp.float32)
    if sink is not None:
        sc = jnp.broadcast_to(sink.reshape(hk, g).astype(jnp.float32)[None, :, :, None, None], s.shape[:-1] + (1,))
        p = jax.nn.softmax(jnp.concatenate([sc, s], axis=-1), axis=-1)[..., 1:]
    else:
        p = jax.nn.softmax(s, axis=-1)
    return jnp.einsum('bhgqk,bkhd->bqhgd', p.astype(v.dtype), v).reshape(Bn, Lc, Hq * d)


def window_attention(q, k, v, k_ctx, v_ctx, sink):
    Bn, L, Hq, d = q.shape
    hk = k.shape[2]
    g = Hq // hk
    nb = L // BLK
    Lc = k_ctx.shape[1]
    qb = jnp.moveaxis(q.reshape(Bn, nb, BLK, hk, g, d), 1, 0) * (d ** -0.5)

    def band(t):
        tp = jnp.pad(t, ((0, 0), (BLK, BLK), (0, 0), (0, 0))).reshape(Bn, nb + 2, BLK, hk, d)
        tb = jnp.concatenate([tp[:, :-2], tp[:, 1:-1], tp[:, 2:]], axis=2)
        return jnp.moveaxis(tb, 1, 0)

    kb, vb = band(k), band(v)
    qpos = jnp.arange(nb)[:, None] * BLK + jnp.arange(BLK)[None]
    kpos = (jnp.arange(nb)[:, None] - 1) * BLK + jnp.arange(3 * BLK)[None]
    valid = ((kpos[:, None, :] >= 0) & (kpos[:, None, :] < L)
             & (jnp.abs(qpos[:, :, None] - kpos[:, None, :]) <= WINDOW))
    sink_logit = sink.reshape(hk, g).astype(jnp.float32)

    def one_block(args):
        qblk, kblk, vblk, vmask = args
        s_c = jnp.einsum('bqhgd,bkhd->bhgqk', qblk, k_ctx).astype(jnp.float32)
        s_w = jnp.where(vmask, jnp.einsum('bqhgd,bkhd->bhgqk', qblk, kblk).astype(jnp.float32), -jnp.inf)
        sc = jnp.broadcast_to(sink_logit[None, :, :, None, None], s_c.shape[:-1] + (1,))
        p = jax.nn.softmax(jnp.concatenate([sc, s_c, s_w], axis=-1), axis=-1).astype(v.dtype)
        return (jnp.einsum('bhgqk,bkhd->bqhgd', p[..., 1:1 + Lc], v_ctx)
                + jnp.einsum('bhgqk,bkhd->bqhgd', p[..., 1 + Lc:], vblk))

    out = lax.map(one_block, (qb, kb, vb, valid))
    return jnp.moveaxis(out, 0, 1).reshape(Bn, L, Hq * d)


def dense_attention(q, k, v, k_ctx, v_ctx):
    Bn, L, Hq, d = q.shape
    hk = k.shape[2]
    g = Hq // hk
    nb = L // BLK
    k_all = jnp.concatenate([k_ctx, k], axis=1)
    v_all = jnp.concatenate([v_ctx, v], axis=1)
    qb = jnp.moveaxis(q.reshape(Bn, nb, BLK, hk, g, d), 1, 0) * (d ** -0.5)

    def one_block(qblk):
        s = jnp.einsum('bqhgd,bkhd->bhgqk', qblk, k_all).astype(jnp.float32)
        p = jax.nn.softmax(s, axis=-1).astype(v.dtype)
        return jnp.einsum('bhgqk,bkhd->bqhgd', p, v_all)

    out = lax.map(one_block, qb)
    return jnp.moveaxis(out, 0, 1).reshape(Bn, L, Hq * d)


def mixer(hl, hc, cos, sin, lb, p, ctx_out):
    dt = hl.dtype
    (ax_l, ay_l, bq_l, bff_l, bfb_l, bi_l, bg_l, cq_l, ck_l, cv_l, dq_l, dk_l, dv_l) = split_in(hl @ p['w_in'])
    (ax_c, ay_c, bq_c, bff_c, bfb_c, bi_c, bg_c, cq_c, ck_c, cv_c, dq_c, dk_c, dv_c) = split_in(hc @ p['w_in'])

    xa_l = conv_centred(ax_l, p['a_conv_w'], p['a_conv_b'])
    xa_c = conv_centred(ax_c, p['a_conv_w'], p['a_conv_b'])

    def a_dir(d, ins, h0):
        (xin,) = ins
        a, u = lru_coeffs(xin, p['a_gate_a_w'][d], p['a_gate_a_b'][d], p['a_gate_x_w'][d],
                          p['a_gate_x_b'][d], p['a_lambda'][d])
        if h0 is None:
            h0 = jnp.zeros_like(u[:, 0])
        h = linear_scan(a, u, h0)
        return h, h[:, -1]

    ha_c, ha_l = bidirectional(a_dir, [(xa_c,), (xa_c,)], [(xa_l,), (xa_l,)], ctx_out)
    ya_l = ha_l.astype(dt) * jax.nn.gelu(ay_l)

    def b_dir(d, ins, S0):
        q, k, v, g = ins
        if S0 is None:
            S0 = jnp.zeros((q.shape[0], B_HEADS, B_DK, B_DV), jnp.float32)
        return hgrn2_chunk_scan(q, k, v, g, S0)

    hb_c, hb_l = bidirectional(b_dir, hgrn_inputs(bq_c, bff_c, bfb_c, bi_c, lb),
                               hgrn_inputs(bq_l, bff_l, bfb_l, bi_l, lb), ctx_out)
    yb_l = hgrn_readout(hb_l, bg_l, p['b_onorm_w'])

    ck_ch, cv_ch = split_heads(ck_c, C_KV), split_heads(cv_c, C_KV)
    yc_l = window_attention(apply_rope(split_heads(cq_l, C_HEADS), cos, sin),
                            apply_rope(split_heads(ck_l, C_KV), cos, sin), split_heads(cv_l, C_KV),
                            ck_ch, cv_ch, p['c_sink'])

    dk_ch = rmsnorm(split_heads(dk_c, D_KV), p['d_knorm_w'])
    dv_ch = split_heads(dv_c, D_KV)
    yd_l = dense_attention(apply_rope(rmsnorm(split_heads(dq_l, D_HEADS), p['d_qnorm_w']), cos, sin),
                           apply_rope(rmsnorm(split_heads(dk_l, D_KV), p['d_knorm_w']), cos, sin),
                           split_heads(dv_l, D_KV), dk_ch, dv_ch)

    out_l = jnp.concatenate([ya_l, yb_l, yc_l, yd_l], axis=-1) @ p['w_out']
    if not ctx_out:
        return out_l, None
    ya_c = ha_c.astype(dt) * jax.nn.gelu(ay_c)
    yb_c = hgrn_readout(hb_c, bg_c, p['b_onorm_w'])
    yc_c = ctx_attention(split_heads(cq_c, C_HEADS), ck_ch, cv_ch, p['c_sink'])
    yd_c = ctx_attention(rmsnorm(split_heads(dq_c, D_HEADS), p['d_qnorm_w']), dk_ch, dv_ch, None)
    out_c = jnp.concatenate([ya_c, yb_c, yc_c, yd_c], axis=-1) @ p['w_out']
    return out_l, out_c


def expert_choice_ffn(h, w_router, w1, w3, w2):
    Bn, N, D = h.shape
    cap = max(1, CAP_FACTOR * N // N_EXPERTS)
    aff = jax.nn.softmax((h @ w_router).astype(jnp.float32), axis=-1)
    gate, idx = lax.top_k(jnp.swapaxes(aff, 1, 2), cap)
    xs = jax.vmap(lambda hb, ib: hb[ib])(h, idx)
    hid = jax.nn.silu(jnp.einsum('becd,edf->becf', xs, w1)) * jnp.einsum('becd,edf->becf', xs, w3)
    y = jnp.einsum('becf,efd->becd', hid, w2) * gate[..., None].astype(h.dtype)
    return jax.vmap(lambda yb, ib: jnp.zeros((N, D), yb.dtype).at[ib].add(yb))(y, idx)


def setup_inputs(seed: int = 0) -> dict:
    key = jax.random.key(seed)
    ks = list(jax.random.split(key, 32))

    def nrm(shape, scale):
        return jax.random.normal(ks.pop(), shape, jnp.float32) * scale

    a0 = jax.random.uniform(ks.pop(), (DEPTH, 2, GROUP_W), jnp.float32, minval=0.9, maxval=0.999)
    pa = a0 ** (1.0 / LRU_C)
    a_lambda = jnp.log(pa) - jnp.log1p(-pa)
    return {
        'x': nrm((BATCH, SEQ, D_MODEL), 1.0),
        'c': nrm((BATCH, D_MODEL), 1.0),
        'ctx': nrm((BATCH, CTX_LEN, D_MODEL), 1.0),
        'c_ctx': nrm((D_MODEL,), 1.0),
        'mod_w': nrm((DEPTH, D_MODEL, 6 * D_MODEL), 0.5 * D_MODEL ** -0.5),
        'mod_b': nrm((DEPTH, 6 * D_MODEL), 0.01),
        'norm1_w': 1.0 + nrm((DEPTH, D_MODEL), 0.02),
        'w_in': nrm((DEPTH, D_MODEL, D_IN), D_MODEL ** -0.5),
        'a_conv_w': nrm((DEPTH, CONV_W, GROUP_W), CONV_W ** -0.5),
        'a_conv_b': nrm((DEPTH, GROUP_W), 0.01),
        'a_gate_a_w': nrm((DEPTH, 2, A_BLOCKS, A_BLK, A_BLK), A_BLK ** -0.5),
        'a_gate_a_b': nrm((DEPTH, 2, GROUP_W), 0.01),
        'a_gate_x_w': nrm((DEPTH, 2, A_BLOCKS, A_BLK, A_BLK), A_BLK ** -0.5),
        'a_gate_x_b': nrm((DEPTH, 2, GROUP_W), 0.01),
        'a_lambda': a_lambda,
        'b_lb_logits': nrm((DEPTH, GROUP_W), 0.5),
        'b_onorm_w': 1.0 + nrm((DEPTH, B_DV), 0.02),
        'c_sink': nrm((DEPTH, C_HEADS), 0.5),
        'd_qnorm_w': 1.0 + nrm((DEPTH, HEAD_DIM), 0.02),
        'd_knorm_w': 1.0 + nrm((DEPTH, HEAD_DIM), 0.02),
        'w_out': nrm((DEPTH, D_MIX, D_MODEL), D_MIX ** -0.5),
        'norm2_w': 1.0 + nrm((DEPTH, D_MODEL), 0.02),
        'router_w': nrm((DEPTH, D_MODEL, N_EXPERTS), D_MODEL ** -0.5),
        'exp_w1': nrm((DEPTH, N_EXPERTS, D_MODEL, D_EXPERT), D_MODEL ** -0.5),
        'exp_w3': nrm((DEPTH, N_EXPERTS, D_MODEL, D_EXPERT), D_MODEL ** -0.5),
        'exp_w2': nrm((DEPTH, N_EXPERTS, D_EXPERT, D_MODEL), D_EXPERT ** -0.5),
        'final_norm_w': 1.0 + nrm((D_MODEL,), 0.02),
    }


def reference(x, c, ctx, c_ctx, mod_w, mod_b, norm1_w, w_in, a_conv_w, a_conv_b, a_gate_a_w, a_gate_a_b,
              a_gate_x_w, a_gate_x_b, a_lambda, b_lb_logits, b_onorm_w, c_sink, d_qnorm_w, d_knorm_w, w_out,
              norm2_w, router_w, exp_w1, exp_w3, exp_w2, final_norm_w):
    L = x.shape[1]
    ROWS = L // GRID_W
    cos, sin = axial_rope(ROWS)
    lb = jnp.cumsum(jax.nn.softmax(b_lb_logits.astype(jnp.float32), axis=0), axis=0)
    lb = lb - lb[0]
    s_lat = jax.nn.silu(c)[:, None, :]
    s_ctx = jax.nn.silu(c_ctx)[None, None, :]
    xl, xc = x, ctx
    for l in range(DEPTH):
        last = l == DEPTH - 1
        p = {'w_in': w_in[l], 'w_out': w_out[l], 'a_conv_w': a_conv_w[l], 'a_conv_b': a_conv_b[l],
             'a_gate_a_w': a_gate_a_w[l], 'a_gate_a_b': a_gate_a_b[l], 'a_gate_x_w': a_gate_x_w[l],
             'a_gate_x_b': a_gate_x_b[l], 'a_lambda': a_lambda[l], 'b_onorm_w': b_onorm_w[l],
             'c_sink': c_sink[l], 'd_qnorm_w': d_qnorm_w[l], 'd_knorm_w': d_knorm_w[l]}
        m_l = jnp.split(s_lat @ mod_w[l] + mod_b[l], 6, axis=-1)
        m_c = jnp.split(s_ctx @ mod_w[l] + mod_b[l], 6, axis=-1)
        hl = modulate(rmsnorm(xl, norm1_w[l]), m_l[0], m_l[1])
        hc = modulate(rmsnorm(xc, norm1_w[l]), m_c[0], m_c[1])
        ol, oc = mixer(hl, hc, cos, sin, lb[l], p, not last)
        xl = xl + m_l[2] * ol
        hl = modulate(rmsnorm(xl, norm2_w[l]), m_l[3], m_l[4])
        xl = xl + m_l[5] * expert_choice_ffn(hl, router_w[l], exp_w1[l], exp_w3[l], exp_w2[l])
        if not last:
            xc = xc + m_c[2] * oc
            hc = modulate(rmsnorm(xc, norm2_w[l]), m_c[3], m_c[4])
            xc = xc + m_c[5] * expert_choice_ffn(hc, router_w[l], exp_w1[l], exp_w3[l], exp_w2[l])
    return rmsnorm(xl, final_norm_w)
```

```python
import functools
import math

import jax
import jax.numpy as jnp
import numpy as np
from jax import lax
from jax.experimental import pallas as pl
from jax.experimental.pallas import tpu as pltpu

D_MODEL = 1024
GRID_W = 64
GROUP_W = D_MODEL // 4
HEAD_DIM = 64
EPS = 1e-6
A_BLOCKS = 4
CONV_W = 4
LRU_C = 8.0
B_HEADS = 4
WINDOW = 128
ROPE_THETA = 10000.0
N_EXPERTS = 16
CAP_FACTOR = 2
D_EXPERT = 1024
D_IN = 2816
N_MOD = 6

LANES = 128
SUBLANES = 8
VMEM_LIMIT_BYTES = 56 * 1024 * 1024

NEG_BIG = -1e30
F32 = jnp.float32
BF16 = jnp.bfloat16


def _cparams(sem):
    return pltpu.CompilerParams(dimension_semantics=sem, vmem_limit_bytes=VMEM_LIMIT_BYTES)


def _dot(a, b):
    return jnp.dot(a, b, preferred_element_type=F32)


def _dot_nt(a, b):
    return lax.dot_general(a, b, (((1,), (1,)), ((), ())), preferred_element_type=F32)


def _silu(x):
    return x * (1.0 / (1.0 + jnp.exp(-x)))


def _sigmoid(x):
    return 1.0 / (1.0 + jnp.exp(-x))


def _iota(shape, dim):
    return lax.broadcasted_iota(jnp.int32, shape, dim)


def _mod_kernel(s_ref, w_ref, b_ref, o_ref):
    s = _silu(s_ref[...])
    o_ref[0] = jnp.dot(s, w_ref[0], precision=lax.Precision.HIGHEST, preferred_element_type=F32) + b_ref[0]


def _modulation(s_rows, mod_w, mod_b):
    depth, d, n = mod_w.shape
    tn = 1536
    return pl.pallas_call(
        _mod_kernel,
        grid=(depth, n // tn),
        in_specs=[pl.BlockSpec((SUBLANES, d), lambda l, j: (0, 0)),
                  pl.BlockSpec((1, d, tn), lambda l, j: (l, 0, j)),
                  pl.BlockSpec((1, 1, tn), lambda l, j: (l, 0, j))],
        out_specs=pl.BlockSpec((1, SUBLANES, tn), lambda l, j: (l, 0, j)),
        out_shape=jax.ShapeDtypeStruct((depth, SUBLANES, n), F32),
        compiler_params=_cparams(("parallel", "parallel")),
        name="modulation",
    )(s_rows, mod_w, mod_b.reshape(depth, 1, n))


def _rope(x, cos, sin_signed):
    w = x.shape[-1]
    lane = _iota(x.shape, 1)
    partner = jnp.where((lane % 32) < 16, pltpu.roll(x, w - 16, 1), pltpu.roll(x, 16, 1))
    return x * cos + partner * sin_signed


def _head_rms(x, bd_mean, w):
    ms = _dot((x * x).astype(BF16), bd_mean)
    return x * lax.rsqrt(ms + EPS) * w


def _pack_qkv(o_ref, q, k, v):
    o_ref[0, :, 0:256] = q.astype(BF16)
    o_ref[0, :, 256:384] = k.astype(BF16)
    o_ref[0, :, 384:512] = pltpu.roll(k, 64, 1).astype(BF16)
    o_ref[0, :, 512:640] = v.astype(BF16)
    o_ref[0, :, 640:768] = pltpu.roll(v, 64, 1).astype(BF16)


def _inproj_kernel(x_ref, nw_ref, sh_ref, sc_ref, w_ref, cos_ref, sin_ref, qw_ref, kw_ref, bd_ref,
                   za_ref, zb_ref, qc_ref, qd_ref):
    x = x_ref[0]
    y = x * lax.rsqrt(jnp.mean(x * x, axis=-1, keepdims=True) + EPS) * nw_ref[...]
    h = y * (1.0 + sc_ref[0]) + sh_ref[0]
    z = _dot(h.astype(BF16), w_ref[...])
    za_ref[0] = z[:, 0:512]
    zb_ref[0] = z[:, 512:1792]
    cos = cos_ref[...]
    sin = sin_ref[...]
    scale = HEAD_DIM ** -0.5
    cq = _rope(z[:, 1792:2048], cos, sin) * scale
    ck = _rope(z[:, 2048:2176], cos[:, 0:128], sin[:, 0:128])
    _pack_qkv(qc_ref, cq, ck, z[:, 2176:2304])
    bd = bd_ref[...]
    dq = _rope(_head_rms(z[:, 2304:2560], bd, qw_ref[...]), cos, sin) * scale
    dk = _rope(_head_rms(z[:, 2560:2688], bd[0:128, 0:128], kw_ref[...]), cos[:, 0:128], sin[:, 0:128])
    _pack_qkv(qd_ref, dq, dk, z[:, 2688:2816])


def _inproj(x, norm_w, shift, scale, w_in_bf, cos, sin, qw, kw, bd_mean, tm):
    b, n, d = x.shape
    row = lambda bi, i: (bi, i, 0)
    const2 = lambda bi, i: (0, 0)
    return pl.pallas_call(
        _inproj_kernel,
        grid=(b, n // tm),
        in_specs=[pl.BlockSpec((1, tm, d), row),
                  pl.BlockSpec((1, d), const2),
                  pl.BlockSpec((1, 1, d), lambda bi, i: (bi, 0, 0)),
                  pl.BlockSpec((1, 1, d), lambda bi, i: (bi, 0, 0)),
                  pl.BlockSpec((d, D_IN), const2),
                  pl.BlockSpec((tm, 256), lambda bi, i: (i, 0)),
                  pl.BlockSpec((tm, 256), lambda bi, i: (i, 0)),
                  pl.BlockSpec((1, 256), const2),
                  pl.BlockSpec((1, 128), const2),
                  pl.BlockSpec((256, 256), const2)],
        out_specs=[pl.BlockSpec((1, tm, 512), row), pl.BlockSpec((1, tm, 1280), row),
                   pl.BlockSpec((1, tm, 768), row), pl.BlockSpec((1, tm, 768), row)],
        out_shape=[jax.ShapeDtypeStruct((b, n, 512), F32), jax.ShapeDtypeStruct((b, n, 1280), F32),
                   jax.ShapeDtypeStruct((b, n, 768), BF16), jax.ShapeDtypeStruct((b, n, 768), BF16)],
        compiler_params=_cparams(("parallel", "parallel")),
        name="inproj",
    )(x, norm_w, shift, scale, w_in_bf, cos, sin, qw, kw, bd_mean)


def _lru_kernel(fc_ref, fp_ref, fn_ref, bc_ref, bp_ref, bn_ref, h0_ref, cw_ref, cb_ref, gw_ref, gb_ref, lam_ref,
                hf_ref, hb_ref, hl_ref, carry_ref, *, t, nt):
    i = pl.program_id(1)

    @pl.when(i == 0)
    def _():
        carry_ref[...] = h0_ref[0]

    row = _iota((t, GROUP_W), 0)
    cw = cw_ref[...]

    def coeffs(cur_ref, prev_ref, next_ref, tile, d):
        prev = jnp.where(tile == 0, 0.0, prev_ref[0])
        nxt = jnp.where(tile == nt - 1, 0.0, next_ref[0])
        ext = jnp.concatenate([prev, cur_ref[0], nxt], axis=0)
        n_ext = t + 2 * SUBLANES
        xa = cb_ref[...] + cw[2:3] * ext[SUBLANES:SUBLANES + t]
        for j, off in ((0, 2), (1, 1), (3, -1)):
            xa = xa + cw[j:j + 1] * pltpu.roll(ext, off % n_ext, 0)[SUBLANES:SUBLANES + t]
        g = _dot(xa.astype(BF16), gw_ref[d]) + gb_ref[d]
        r = _sigmoid(g[:, 0:GROUP_W])
        gi = _sigmoid(g[:, GROUP_W:2 * GROUP_W])
        lam = lam_ref[d]
        softplus = jnp.maximum(-lam, 0.0) + jnp.log(1.0 + jnp.exp(-jnp.abs(lam)))
        log_a = (-LRU_C) * r * softplus
        a = jnp.exp(log_a)
        u = jnp.sqrt(1.0 - jnp.exp(2.0 * log_a)) * (gi * xa)
        return a, u

    a, u = coeffs(fc_ref, fp_ref, fn_ref, i, 0)
    s = 1
    while s < t:
        keep = row >= s
        a_sh = jnp.where(keep, pltpu.roll(a, s, 0), 1.0)
        u_sh = jnp.where(keep, pltpu.roll(u, s, 0), 0.0)
        u = a * u_sh + u
        a = a * a_sh
        s *= 2
    h = u + a * carry_ref[0:1]
    hf_ref[0] = h
    carry_ref[0:1] = hf_ref[0, t - 1:t, :]

    a, u = coeffs(bc_ref, bp_ref, bn_ref, nt - 1 - i, 1)
    s = 1
    while s < t:
        keep = row < t - s
        a_sh = jnp.where(keep, pltpu.roll(a, t - s, 0), 1.0)
        u_sh = jnp.where(keep, pltpu.roll(u, t - s, 0), 0.0)
        u = a * u_sh + u
        a = a * a_sh
        s *= 2
    h = u + a * carry_ref[1:2]
    hb_ref[0] = h
    carry_ref[1:2] = hb_ref[0, 0:1, :]

    @pl.when(i == nt - 1)
    def _():
        hl_ref[0] = carry_ref[...]


def _lru(za, h0, conv_w, conv_b, gate_w, gate_b, lam, t):
    b, n, _ = za.shape
    nt = n // t
    t8 = t // SUBLANES
    n8 = n // SUBLANES
    cur_f = lambda bi, i: (bi, i, 0)
    prev_f = lambda bi, i: (bi, jnp.maximum(i * t8 - 1, 0), 0)
    next_f = lambda bi, i: (bi, jnp.minimum((i + 1) * t8, n8 - 1), 0)
    cur_b = lambda bi, i: (bi, nt - 1 - i, 0)
    prev_b = lambda bi, i: (bi, jnp.maximum((nt - 1 - i) * t8 - 1, 0), 0)
    next_b = lambda bi, i: (bi, jnp.minimum((nt - i) * t8, n8 - 1), 0)
    c2 = lambda bi, i: (0, 0)
    c3 = lambda bi, i: (0, 0, 0)
    halo = (1, SUBLANES, GROUP_W)
    return pl.pallas_call(
        functools.partial(_lru_kernel, t=t, nt=nt),
        grid=(b, nt),
        in_specs=[pl.BlockSpec((1, t, GROUP_W), cur_f), pl.BlockSpec(halo, prev_f), pl.BlockSpec(halo, next_f),
                  pl.BlockSpec((1, t, GROUP_W), cur_b), pl.BlockSpec(halo, prev_b), pl.BlockSpec(halo, next_b),
                  pl.BlockSpec((1, SUBLANES, GROUP_W), lambda bi, i: (bi, 0, 0)),
                  pl.BlockSpec((SUBLANES, GROUP_W), c2), pl.BlockSpec((1, GROUP_W), c2),
                  pl.BlockSpec((2, GROUP_W, 2 * GROUP_W), c3), pl.BlockSpec((2, 1, 2 * GROUP_W), c3),
                  pl.BlockSpec((2, 1, GROUP_W), c3)],
        out_specs=[pl.BlockSpec((1, t, GROUP_W), cur_f), pl.BlockSpec((1, t, GROUP_W), cur_b),
                   pl.BlockSpec((1, SUBLANES, GROUP_W), lambda bi, i: (bi, 0, 0))],
        out_shape=[jax.ShapeDtypeStruct((b, n, GROUP_W), F32), jax.ShapeDtypeStruct((b, n, GROUP_W), F32),
                   jax.ShapeDtypeStruct((b, SUBLANES, GROUP_W), F32)],
        scratch_shapes=[pltpu.VMEM((SUBLANES, GROUP_W), F32)],
        compiler_params=_cparams(("parallel", "arbitrary")),
        name="rglru",
    )(za, za, za, za, za, za, h0, conv_w, conv_b, gate_w, gate_b, lam)


B_CHUNK = 128
B_SUB = 32


def _hgrn_chunk(q_raw, f_raw, v, lb, st_ref, d, b_ref, k_ref, v_ref, bd_ones, reverse):
    t = B_CHUNK
    n_sub = t // B_SUB
    q = _silu(q_raw) * (GROUP_W // B_HEADS) ** -0.5
    log_sig = jnp.minimum(f_raw, 0.0) - jnp.log(1.0 + jnp.exp(-jnp.abs(f_raw)))
    log_lb = jnp.log(lb)
    y = jnp.log(1.0 - lb) + log_sig
    log_f = jnp.maximum(log_lb, y) + jnp.log(1.0 + jnp.exp(-jnp.abs(log_lb - y)))
    k = (1.0 - lb) * _sigmoid(-f_raw)
    row = _iota((t, GROUP_W), 0)
    b = log_f
    s = 1
    while s < t:
        if reverse:
            b = b + jnp.where(row < t - s, pltpu.roll(b, t - s, 0), 0.0)
        else:
            b = b + jnp.where(row >= s, pltpu.roll(b, s, 0), 0.0)
        s *= 2
    b_ref[...] = b
    k_ref[...] = k
    v_ref[...] = v
    edge = 0 if reverse else t - 1
    b_edge = b_ref[edge:edge + 1, :]
    st = st_ref[d]
    lane = _iota((B_SUB, GROUP_W), 1)
    head_masks = [(lane >= 64 * hh) & (lane < 64 * (hh + 1)) for hh in range(B_HEADS)]
    trow = _iota((B_SUB, GROUP_W), 0)
    srow = _iota((t, GROUP_W), 0)

    o_inter = _dot_nt((q * jnp.exp(b)).astype(BF16), st.astype(BF16))
    outs = []
    for i in range(n_sub):
        lo = i * B_SUB
        qb = q[lo:lo + B_SUB]
        bb = b[lo:lo + B_SUB]
        acc = o_inter[lo:lo + B_SUB]
        has_off = (i < n_sub - 1) if reverse else (i > 0)
        if has_off:
            ref_row = lo + B_SUB if reverse else lo - 1
            r_i = b_ref[ref_row:ref_row + 1, :]
            qh = qb * jnp.exp(bb - r_i)
            key_ok = (srow >= lo + B_SUB) if reverse else (srow < lo)
            kh = jnp.where(key_ok, k * jnp.exp(jnp.minimum(r_i - b, 0.0)), 0.0)
            qstack = jnp.concatenate([jnp.where(m, qh, 0.0) for m in head_masks], axis=0).astype(BF16)
            att = _dot_nt(qstack, kh.astype(BF16))
            res = _dot(att.astype(BF16), v.astype(BF16))
            for hh in range(B_HEADS):
                acc = acc + jnp.where(head_masks[hh], res[hh * B_SUB:(hh + 1) * B_SUB], 0.0)
        pieces = []
        for sl in range(B_SUB):
            srow_b = b_ref[lo + sl:lo + sl + 1, :]
            srow_k = k_ref[lo + sl:lo + sl + 1, :]
            ok = (trow <= sl) if reverse else (trow >= sl)
            e = jnp.exp(jnp.where(ok, bb - srow_b, NEG_BIG)) * (qb * srow_k)
            pieces.append(e.astype(BF16))
        g = _dot(jnp.concatenate(pieces, axis=0), bd_ones)
        for sl in range(B_SUB):
            acc = acc + g[sl * B_SUB:(sl + 1) * B_SUB] * v_ref[lo + sl:lo + sl + 1, :]
        outs.append(acc)
    kt = k * jnp.exp(b_edge - b)
    upd = _dot(jnp.transpose(v).astype(BF16), kt.astype(BF16))
    vrow = _iota((GROUP_W, GROUP_W), 0) // 64
    kcol = _iota((GROUP_W, GROUP_W), 1) // 64
    st_ref[d] = st * jnp.exp(b_edge) + jnp.where(vrow == kcol, upd, 0.0)
    return jnp.concatenate(outs, axis=0)


def _hgrn_kernel(qf_ref, ff_ref, vf_ref, qb_ref, fb_ref, vb_ref, s0_ref, lb_ref, bd_ref,
                 of_ref, ob_ref, sl_ref, st_ref, b_ref, k_ref, v_ref, *, nt):
    i = pl.program_id(1)

    @pl.when(i == 0)
    def _():
        st_ref[...] = s0_ref[0]

    lb = lb_ref[...]
    bd = bd_ref[...]
    of_ref[0] = _hgrn_chunk(qf_ref[0], ff_ref[0], vf_ref[0], lb, st_ref, 0, b_ref, k_ref, v_ref, bd, False)
    ob_ref[0] = _hgrn_chunk(qb_ref[0], fb_ref[0], vb_ref[0], lb, st_ref, 1, b_ref, k_ref, v_ref, bd, True)

    @pl.when(i == nt - 1)
    def _():
        sl_ref[0] = st_ref[...]


def _hgrn(zb, s0, lb, bd_ones):
    b, n, _ = zb.shape
    t = B_CHUNK
    nt = n // t
    blk = (1, t, GROUP_W)
    fwd = lambda col: (lambda bi, i: (bi, i, col))
    bwd = lambda col: (lambda bi, i: (bi, nt - 1 - i, col))
    state_spec = pl.BlockSpec((1, 2, GROUP_W, GROUP_W), lambda bi, i: (bi, 0, 0, 0))
    return pl.pallas_call(
        functools.partial(_hgrn_kernel, nt=nt),
        grid=(b, nt),
        in_specs=[pl.BlockSpec(blk, fwd(0)), pl.BlockSpec(blk, fwd(1)), pl.BlockSpec(blk, fwd(3)),
                  pl.BlockSpec(blk, bwd(0)), pl.BlockSpec(blk, bwd(2)), pl.BlockSpec(blk, bwd(3)),
                  state_spec,
                  pl.BlockSpec((1, GROUP_W), lambda bi, i: (0, 0)),
                  pl.BlockSpec((GROUP_W, GROUP_W), lambda bi, i: (0, 0))],
        out_specs=[pl.BlockSpec(blk, fwd(0)), pl.BlockSpec(blk, bwd(0)), state_spec],
        out_shape=[jax.ShapeDtypeStruct((b, n, GROUP_W), F32), jax.ShapeDtypeStruct((b, n, GROUP_W), F32),
                   jax.ShapeDtypeStruct((b, 2, GROUP_W, GROUP_W), F32)],
        scratch_shapes=[pltpu.VMEM((2, GROUP_W, GROUP_W), F32), pltpu.VMEM((t, GROUP_W), F32),
                        pltpu.VMEM((t, GROUP_W), F32), pltpu.VMEM((t, GROUP_W), F32)],
        compiler_params=_cparams(("parallel", "arbitrary")),
        name="hgrn2",
    )(zb, zb, zb, zb, zb, zb, s0, lb, bd_ones)


def _stack_q(q):
    tq = q.shape[0]
    lane = _iota((tq, LANES), 1)
    lo = lane < 64
    q0 = q[:, 0:128]
    q1 = q[:, 128:256]
    zero = jnp.zeros_like(q0)
    s1 = jnp.concatenate([jnp.where(lo, q0, zero), jnp.where(lo, zero, q1)], axis=0)
    s2 = jnp.concatenate([jnp.where(lo, zero, q0), jnp.where(lo, q1, zero)], axis=0)
    return s1, s2


def _unstack_out(r1, r2, tq):
    lane = _iota((tq, LANES), 1)
    lo = lane < 64
    o0 = jnp.where(lo, r1[0:tq], r2[0:tq])
    o1 = jnp.where(lo, r2[tq:2 * tq], r1[tq:2 * tq])
    return jnp.concatenate([o0, o1], axis=1)


def _sink_cols(sink_ref, tq):
    row = _iota((2 * tq, 1), 0)
    c1 = jnp.where(row < tq, sink_ref[0], sink_ref[3])
    c2 = jnp.where(row < tq, sink_ref[1], sink_ref[2])
    return c1, c2


def _window_kernel(sink_ref, q_ref, kp_ref, kc_ref, kn_ref, vp_ref, vc_ref, vn_ref, kx_ref, vx_ref, o_ref,
                   *, tq, n):
    i = pl.program_id(1)
    s1, s2 = _stack_q(q_ref[0])
    kband = jnp.concatenate([kp_ref[0], kc_ref[0], kn_ref[0]], axis=0)
    vband = jnp.concatenate([vp_ref[0], vc_ref[0], vn_ref[0]], axis=0)
    nband = tq + 2 * WINDOW
    qpos = i * tq + (_iota((2 * tq, nband), 0) % tq)
    kpos = i * tq - WINDOW + _iota((2 * tq, nband), 1)
    ok = (kpos >= 0) & (kpos < n) & (jnp.abs(qpos - kpos) <= WINDOW)
    sinks = _sink_cols(sink_ref, tq)
    res = []
    for qs, half, sink in ((s1, 0, sinks[0]), (s2, 1, sinks[1])):
        kx = kx_ref[0][:, half * 128:(half + 1) * 128]
        vx = vx_ref[0][:, half * 128:(half + 1) * 128]
        sc = _dot_nt(qs, kx)
        sw = jnp.where(ok, _dot_nt(qs, kband[:, half * 128:(half + 1) * 128]), NEG_BIG)
        m = jnp.maximum(jnp.maximum(jnp.max(sc, axis=-1, keepdims=True), jnp.max(sw, axis=-1, keepdims=True)), sink)
        pc = jnp.exp(sc - m)
        pw = jnp.exp(sw - m)
        den = jnp.sum(pc, axis=-1, keepdims=True) + jnp.sum(pw, axis=-1, keepdims=True) + jnp.exp(sink - m)
        num = _dot(pc.astype(BF16), vx) + _dot(pw.astype(BF16), vband[:, half * 128:(half + 1) * 128])
        res.append(num / den)
    o_ref[0] = _unstack_out(res[0], res[1], tq).astype(BF16)


def _window_attention(qkv, qkv_ctx, sink, tq):
    b, n, _ = qkv.shape
    lc = qkv_ctx.shape[1]
    w = WINDOW
    r = tq // w
    nw = n // w
    prev = lambda col: (lambda bi, i, s: (bi, jnp.maximum(i * r - 1, 0), col))
    cur = lambda col: (lambda bi, i, s: (bi, i, col))
    nxt = lambda col: (lambda bi, i, s: (bi, jnp.minimum((i + 1) * r, nw - 1), col))
    ctx = lambda col: (lambda bi, i, s: (bi, 0, col))
    grid_spec = pltpu.PrefetchScalarGridSpec(
        num_scalar_prefetch=1,
        grid=(b, n // tq),
        in_specs=[pl.BlockSpec((1, tq, 256), cur(0)),
                  pl.BlockSpec((1, w, 256), prev(1)), pl.BlockSpec((1, tq, 256), cur(1)),
                  pl.BlockSpec((1, w, 256), nxt(1)),
                  pl.BlockSpec((1, w, 256), prev(2)), pl.BlockSpec((1, tq, 256), cur(2)),
                  pl.BlockSpec((1, w, 256), nxt(2)),
                  pl.BlockSpec((1, lc, 256), ctx(1)), pl.BlockSpec((1, lc, 256), ctx(2))],
        out_specs=pl.BlockSpec((1, tq, 256), cur(0)),
    )
    return pl.pallas_call(
        functools.partial(_window_kernel, tq=tq, n=n),
        grid_spec=grid_spec,
        out_shape=jax.ShapeDtypeStruct((b, n, 256), BF16),
        compiler_params=_cparams(("parallel", "parallel")),
        name="window_attn",
    )(sink, qkv, qkv, qkv, qkv, qkv, qkv, qkv, qkv_ctx, qkv_ctx)


def _ctx_attn_kernel(sink_ref, q_ref, k_ref, v_ref, o_ref, *, tq, use_sink):
    s1, s2 = _stack_q(q_ref[0])
    sinks = _sink_cols(sink_ref, tq)
    res = []
    for qs, half, sink in ((s1, 0, sinks[0]), (s2, 1, sinks[1])):
        kx = k_ref[0][:, half * 128:(half + 1) * 128]
        vx = v_ref[0][:, half * 128:(half + 1) * 128]
        sc = _dot_nt(qs, kx)
        m = jnp.max(sc, axis=-1, keepdims=True)
        if use_sink:
            m = jnp.maximum(m, sink)
        p = jnp.exp(sc - m)
        den = jnp.sum(p, axis=-1, keepdims=True)
        if use_sink:
            den = den + jnp.exp(sink - m)
        res.append(_dot(p.astype(BF16), vx) / den)
    o_ref[0] = _unstack_out(res[0], res[1], tq).astype(BF16)


def _ctx_attention(qkv_ctx, sink, use_sink):
    b, lc, _ = qkv_ctx.shape
    col = lambda c: (lambda bi, s: (bi, 0, c))
    grid_spec = pltpu.PrefetchScalarGridSpec(
        num_scalar_prefetch=1,
        grid=(b,),
        in_specs=[pl.BlockSpec((1, lc, 256), col(0)), pl.BlockSpec((1, lc, 256), col(1)),
                  pl.BlockSpec((1, lc, 256), col(2))],
        out_specs=pl.BlockSpec((1, lc, 256), col(0)),
    )
    return pl.pallas_call(
        functools.partial(_ctx_attn_kernel, tq=lc, use_sink=use_sink),
        grid_spec=grid_spec,
        out_shape=jax.ShapeDtypeStruct((b, lc, 256), BF16),
        compiler_params=_cparams(("parallel",)),
        name="ctx_attn",
    )(sink, qkv_ctx, qkv_ctx, qkv_ctx)


def _dense_kernel(q_ref, k_ref, v_ref, kx_ref, vx_ref, o_ref, qs_ref, m_ref, l_ref, acc_ref, *, tq, tk, ck, nk):
    ki = pl.program_id(2)

    def update(half, kc, vc):
        s = _dot_nt(qs_ref[half], kc)
        m_old = m_ref[half]
        m_new = jnp.maximum(m_old, jnp.max(s, axis=-1, keepdims=True))
        alpha = jnp.exp(m_old - m_new)
        p = jnp.exp(s - m_new)
        l_ref[half] = alpha * l_ref[half] + jnp.sum(p, axis=-1, keepdims=True)
        acc_ref[half] = alpha * acc_ref[half] + _dot(p.astype(BF16), vc)
        m_ref[half] = m_new

    @pl.when(ki == 0)
    def _():
        s1, s2 = _stack_q(q_ref[0])
        qs_ref[0] = s1
        qs_ref[1] = s2
        m_ref[...] = jnp.full(m_ref.shape, NEG_BIG, F32)
        l_ref[...] = jnp.zeros(l_ref.shape, F32)
        acc_ref[...] = jnp.zeros(acc_ref.shape, F32)
        for half in range(2):
            update(half, kx_ref[0][:, half * 128:(half + 1) * 128], vx_ref[0][:, half * 128:(half + 1) * 128])

    for c in range(tk // ck):
        for half in range(2):
            update(half, k_ref[0, c * ck:(c + 1) * ck, half * 128:(half + 1) * 128],
                   v_ref[0, c * ck:(c + 1) * ck, half * 128:(half + 1) * 128])

    @pl.when(ki == nk - 1)
    def _():
        r1 = acc_ref[0] / l_ref[0]
        r2 = acc_ref[1] / l_ref[1]
        o_ref[0] = _unstack_out(r1, r2, tq).astype(BF16)


def _dense_attention(qkv, qkv_ctx, tq, tk, ck):
    b, n, _ = qkv.shape
    lc = qkv_ctx.shape[1]
    nk = n // tk
    return pl.pallas_call(
        functools.partial(_dense_kernel, tq=tq, tk=tk, ck=ck, nk=nk),
        grid=(b, n // tq, nk),
        in_specs=[pl.BlockSpec((1, tq, 256), lambda bi, qi, ki: (bi, qi, 0)),
                  pl.BlockSpec((1, tk, 256), lambda bi, qi, ki: (bi, ki, 1)),
                  pl.BlockSpec((1, tk, 256), lambda bi, qi, ki: (bi, ki, 2)),
                  pl.BlockSpec((1, lc, 256), lambda bi, qi, ki: (bi, 0, 1)),
                  pl.BlockSpec((1, lc, 256), lambda bi, qi, ki: (bi, 0, 2))],
        out_specs=pl.BlockSpec((1, tq, 256), lambda bi, qi, ki: (bi, qi, 0)),
        out_shape=jax.ShapeDtypeStruct((b, n, 256), BF16),
        scratch_shapes=[pltpu.VMEM((2, 2 * tq, LANES), BF16), pltpu.VMEM((2, 2 * tq, 1), F32),
                        pltpu.VMEM((2, 2 * tq, 1), F32), pltpu.VMEM((2, 2 * tq, LANES), F32)],
        compiler_params=_cparams(("parallel", "parallel", "arbitrary")),
        name="dense_attn",
    )(qkv, qkv, qkv, qkv_ctx, qkv_ctx)


def _gelu_tanh(x):
    return 0.5 * x * (1.0 + jnp.tanh(math.sqrt(2.0 / math.pi) * (x + 0.044715 * (x * x * x))))


def _outproj_kernel(hf_ref, hb_ref, ay_ref, of_ref, ob_ref, bg_ref, yc_ref, yd_ref, x_ref, w_ref, m_ref, nw_ref,
                    ow_ref, bd_ref, rw_ref, xo_ref, h2_ref, aff_ref):
    ya = (hf_ref[0] + hb_ref[0]) * _gelu_tanh(ay_ref[0])
    o = of_ref[0] + ob_ref[0]
    ms = _dot((o * o).astype(BF16), bd_ref[...])
    yb = o * lax.rsqrt(ms + EPS) * ow_ref[...] * _silu(bg_ref[0])
    y = jnp.concatenate([ya.astype(BF16), yb.astype(BF16), yc_ref[0], yd_ref[0]], axis=1)
    out = _dot(y, w_ref[...])
    m = m_ref[0]
    x = x_ref[0] + m[0:1] * out
    xo_ref[0] = x
    hn = x * lax.rsqrt(jnp.mean(x * x, axis=-1, keepdims=True) + EPS) * nw_ref[...]
    h2 = hn * (1.0 + m[2:3]) + m[1:2]
    h2_ref[0] = h2
    logits = lax.dot_general(rw_ref[...], h2, (((1,), (1,)), ((), ())), precision=lax.Precision.HIGHEST,
                             preferred_element_type=F32)
    mx = jnp.max(logits, axis=0, keepdims=True)
    e = jnp.exp(logits - mx)
    aff_ref[0] = e / jnp.sum(e, axis=0, keepdims=True)


def _outproj(hf, hb, za, of, ob, zb, yc, yd, x, w_out_bf, mrows, norm2_w, onorm_w, bd_mean, router_t, tm):
    b, n, d = x.shape
    row = lambda col: (lambda bi, i: (bi, i, col))
    c2 = lambda bi, i: (0, 0)
    g = (1, tm, GROUP_W)
    return pl.pallas_call(
        _outproj_kernel,
        grid=(b, n // tm),
        in_specs=[pl.BlockSpec(g, row(0)), pl.BlockSpec(g, row(0)), pl.BlockSpec(g, row(1)),
                  pl.BlockSpec(g, row(0)), pl.BlockSpec(g, row(0)), pl.BlockSpec(g, row(4)),
                  pl.BlockSpec(g, row(0)), pl.BlockSpec(g, row(0)),
                  pl.BlockSpec((1, tm, d), row(0)),
                  pl.BlockSpec((d, d), c2),
                  pl.BlockSpec((1, SUBLANES, d), lambda bi, i: (bi, 0, 0)),
                  pl.BlockSpec((1, d), c2), pl.BlockSpec((1, GROUP_W), c2),
                  pl.BlockSpec((GROUP_W, GROUP_W), c2), pl.BlockSpec((N_EXPERTS, d), c2)],
        out_specs=[pl.BlockSpec((1, tm, d), row(0)), pl.BlockSpec((1, tm, d), row(0)),
                   pl.BlockSpec((1, N_EXPERTS, tm), lambda bi, i: (bi, 0, i))],
        out_shape=[jax.ShapeDtypeStruct((b, n, d), F32), jax.ShapeDtypeStruct((b, n, d), F32),
                   jax.ShapeDtypeStruct((b, N_EXPERTS, n), F32)],
        compiler_params=_cparams(("parallel", "parallel")),
        name="outproj",
    )(hf, hb, za, of, ob, zb, yc, yd, x, w_out_bf, mrows, norm2_w, onorm_w, bd_mean, router_t)


def _topk_kernel(aff_ref, idx_ref, gate_ref, *, gp, cap):
    x_all = aff_ref[0]
    xb_all = pltpu.bitcast(x_all, jnp.int32)
    kf = float(cap)

    lo = jnp.zeros((N_EXPERTS, 1, 1), jnp.int32)
    hi = jnp.full((N_EXPERTS, 1, 1), 0x7F800000, jnp.int32)

    def bisect(_, carry):
        lo, hi = carry
        mid = lo + lax.shift_right_logical(hi - lo, 1)
        cnt = jnp.sum(jnp.sum(jnp.where(xb_all >= mid, 1.0, 0.0), axis=2, keepdims=True), axis=1, keepdims=True)
        ge = cnt >= kf
        return jnp.where(ge, mid, lo), jnp.where(ge, hi, mid)

    lo, hi = lax.fori_loop(0, 31, bisect, (lo, hi))

    upper = jnp.where(_iota((LANES, LANES), 0) <= _iota((LANES, LANES), 1), 1.0, 0.0).astype(BF16)
    strict = jnp.where(_iota((gp, gp), 1) < _iota((gp, gp), 0), 1.0, 0.0).astype(BF16)
    g_col = _iota((gp, 1), 0).astype(F32)
    j_col = _iota((LANES, 1), 0).astype(F32)
    slot = _iota((1, cap), 1).astype(F32)

    def prefix(mask_f):
        within = _dot(mask_f.astype(BF16), upper)
        before = jnp.sum(_dot(strict, mask_f.astype(BF16)), axis=1, keepdims=True)
        return within, before

    for e in range(N_EXPERTS):
        x = x_all[e]
        xb = xb_all[e]
        thr = lo[e]
        gt = jnp.where(xb > thr, 1.0, 0.0)
        eq = jnp.where(xb == thr, 1.0, 0.0)
        need = kf - jnp.sum(jnp.sum(gt, axis=1, keepdims=True), axis=0, keepdims=True)
        eq_within, eq_before = prefix(eq)
        tie_rank = eq_before + eq_within - eq
        sel = jnp.maximum(gt, jnp.where(tie_rank < need, eq, 0.0))
        within, before = prefix(sel)
        incl = before + within[:, LANES - 1:LANES]
        g_of_s = jnp.sum(jnp.where(incl <= slot, 1.0, 0.0), axis=0, keepdims=True)
        base = jnp.max(jnp.where(before <= slot, before, 0.0), axis=0, keepdims=True)
        rank = slot - base
        onehot = jnp.where(g_col == g_of_s, 1.0, 0.0).astype(BF16)
        counts = _dot(jnp.transpose(within).astype(BF16), onehot)
        j_of_s = jnp.sum(jnp.where(counts <= rank, 1.0, 0.0), axis=0, keepdims=True)
        idx_ref[0, e] = (g_of_s * float(LANES) + j_of_s).astype(jnp.int32)
        xt = jnp.transpose(x)
        x_hi = xt.astype(BF16)
        r1 = xt - x_hi.astype(F32)
        x_mid = r1.astype(BF16)
        x_lo = (r1 - x_mid.astype(F32)).astype(BF16)
        vals = _dot(x_hi, onehot) + _dot(x_mid, onehot) + _dot(x_lo, onehot)
        gate_ref[0, e] = jnp.sum(jnp.where(j_col == j_of_s, vals, 0.0), axis=0, keepdims=True)


def _route(aff_t, n_tokens, cap):
    b = aff_t.shape[0]
    g = n_tokens // LANES
    gp = -(-g // LANES) * LANES
    aff4 = aff_t.reshape(b, N_EXPERTS, g, LANES)
    if gp != g:
        aff4 = jnp.pad(aff4, ((0, 0), (0, 0), (0, gp - g), (0, 0)), constant_values=-1.0)
    return pl.pallas_call(
        functools.partial(_topk_kernel, gp=gp, cap=cap),
        grid=(b,),
        in_specs=[pl.BlockSpec((1, N_EXPERTS, gp, LANES), lambda bi: (bi, 0, 0, 0))],
        out_specs=[pl.BlockSpec((1, N_EXPERTS, 1, cap), lambda bi: (bi, 0, 0, 0)),
                   pl.BlockSpec((1, N_EXPERTS, 1, cap), lambda bi: (bi, 0, 0, 0))],
        out_shape=[jax.ShapeDtypeStruct((b, N_EXPERTS, 1, cap), jnp.int32),
                   jax.ShapeDtypeStruct((b, N_EXPERTS, 1, cap), F32)],
        compiler_params=_cparams(("parallel",)),
        name="route_topk",
    )(aff4)


def _ffn_kernel(idx_ref, h_hbm, acc_in, gate_ref, m_ref, w1_ref, w3_ref, w2_ref, acc_out,
                xbuf, abuf, sem_x, sem_a, sem_o, *, cap, rows):
    b = pl.program_id(1)
    del acc_in

    def row_copy(src, dst, tok, r, sem):
        return pltpu.make_async_copy(src.at[b, pl.ds(tok, 1), :], dst.at[pl.ds(r, 1), :], sem)

    def chunk(c, _):
        base = c * rows

        def start(r, _):
            tok = idx_ref[0, 0, base + r]
            row_copy(h_hbm, xbuf, tok, r, sem_x).start()
            row_copy(acc_out, abuf, tok, r, sem_a).start()
            return 0

        lax.fori_loop(0, rows, start, 0)

        def wait_x(r, _):
            row_copy(h_hbm, xbuf, 0, r, sem_x).wait()
            return 0

        lax.fori_loop(0, rows, wait_x, 0)
        xb = xbuf[...].astype(BF16)
        hid = _silu(_dot(xb, w1_ref[0])) * _dot(xb, w3_ref[0])
        y = _dot(hid.astype(BF16), w2_ref[0])
        y = y * gate_ref[0, 0, pl.ds(base, rows), :] * m_ref[0]

        def wait_a(r, _):
            row_copy(acc_out, abuf, 0, r, sem_a).wait()
            return 0

        lax.fori_loop(0, rows, wait_a, 0)
        abuf[...] = abuf[...] + y

        def put(r, _):
            tok = idx_ref[0, 0, base + r]
            pltpu.make_async_copy(abuf.at[pl.ds(r, 1), :], acc_out.at[b, pl.ds(tok, 1), :], sem_o).start()
            return 0

        lax.fori_loop(0, rows, put, 0)

        def wait_o(r, _):
            pltpu.make_async_copy(abuf.at[pl.ds(r, 1), :], acc_out.at[b, pl.ds(0, 1), :], sem_o).wait()
            return 0

        lax.fori_loop(0, rows, wait_o, 0)
        return 0

    lax.fori_loop(0, cap // rows, chunk, 0)


def _expert_ffn(idx, gate, h2, acc, mgate, w1, w3, w2):
    b, n, d = h2.shape
    cap = idx.shape[-1]
    rows = min(cap, 256)
    return pl.pallas_call(
        functools.partial(_ffn_kernel, cap=cap, rows=rows),
        grid=(N_EXPERTS, b),
        in_specs=[pl.BlockSpec((1, 1, cap), lambda e, bi: (bi * N_EXPERTS + e, 0, 0), memory_space=pltpu.SMEM),
                  pl.BlockSpec(memory_space=pl.ANY),
                  pl.BlockSpec(memory_space=pl.ANY),
                  pl.BlockSpec((1, 1, cap, 1), lambda e, bi: (bi, e, 0, 0)),
                  pl.BlockSpec((1, 1, d), lambda e, bi: (bi, 0, 0)),
                  pl.BlockSpec((1, d, D_EXPERT), lambda e, bi: (e, 0, 0)),
                  pl.BlockSpec((1, d, D_EXPERT), lambda e, bi: (e, 0, 0)),
                  pl.BlockSpec((1, D_EXPERT, d), lambda e, bi: (e, 0, 0))],
        out_specs=pl.BlockSpec(memory_space=pl.ANY),
        out_shape=jax.ShapeDtypeStruct((b, n, d), F32),
        scratch_shapes=[pltpu.VMEM((rows, d), F32), pltpu.VMEM((rows, d), F32),
                        pltpu.SemaphoreType.DMA, pltpu.SemaphoreType.DMA, pltpu.SemaphoreType.DMA],
        input_output_aliases={2: 0},
        compiler_params=pltpu.CompilerParams(dimension_semantics=("arbitrary", "arbitrary"),
                                             vmem_limit_bytes=VMEM_LIMIT_BYTES),
        name="expert_ffn",
    )(idx, h2, acc, gate, mgate, w1, w3, w2)


def _final_norm_kernel(x_ref, w_ref, o_ref):
    x = x_ref[0]
    o_ref[0] = x * lax.rsqrt(jnp.mean(x * x, axis=-1, keepdims=True) + EPS) * w_ref[...]


def _final_norm(x, w, tm):
    b, n, d = x.shape
    return pl.pallas_call(
        _final_norm_kernel,
        grid=(b, n // tm),
        in_specs=[pl.BlockSpec((1, tm, d), lambda bi, i: (bi, i, 0)), pl.BlockSpec((1, d), lambda bi, i: (0, 0))],
        out_specs=pl.BlockSpec((1, tm, d), lambda bi, i: (bi, i, 0)),
        out_shape=jax.ShapeDtypeStruct((b, n, d), F32),
        compiler_params=_cparams(("parallel", "parallel")),
        name="final_norm",
    )(x, w)


def _rope_tables(n):
    half = HEAD_DIM // 2
    n_freq = half // 2
    inv = ROPE_THETA ** (-jnp.arange(n_freq, dtype=F32) * 2.0 / half)
    rows = n // GRID_W
    r = jnp.repeat(jnp.arange(rows, dtype=F32), GRID_W)
    col = jnp.tile(jnp.arange(GRID_W, dtype=F32), rows)
    ang = jnp.stack([r[:, None] * inv, col[:, None] * inv], axis=1)
    cos = jnp.cos(ang)[:, :, None, :]
    sin = jnp.sin(ang)[:, :, None, :]
    cos_h = jnp.broadcast_to(cos, (n, 2, 2, n_freq)).reshape(n, HEAD_DIM)
    sin_h = (sin * jnp.array([-1.0, 1.0], F32)[None, None, :, None]).reshape(n, HEAD_DIM)
    return jnp.tile(cos_h, (1, 4)), jnp.tile(sin_h, (1, 4))


def _block_diag(w):
    nb, bi, bj = w.shape
    eye = jnp.eye(nb, dtype=w.dtype)
    return (eye[:, None, :, None] * w[:, :, None, :]).reshape(nb * bi, nb * bj)


def _tile_for(n, pref):
    t = pref
    while n % t:
        t //= 2
    return t


def kernel(x, c, ctx, c_ctx, mod_w, mod_b, norm1_w, w_in, a_conv_w, a_conv_b, a_gate_a_w, a_gate_a_b, a_gate_x_w,
           a_gate_x_b, a_lambda, b_lb_logits, b_onorm_w, c_sink, d_qnorm_w, d_knorm_w, w_out, norm2_w, router_w,
           exp_w1, exp_w3, exp_w2, final_norm_w):
    bsz, n, d = x.shape
    lc = ctx.shape[1]
    depth = mod_w.shape[0]
    cos, sin = _rope_tables(n)
    cos_c = jnp.ones((lc, 256), F32)
    sin_c = jnp.zeros((lc, 256), F32)
    lb_all = jnp.cumsum(jax.nn.softmax(b_lb_logits.astype(F32), axis=0), axis=0)
    lb_all = lb_all - lb_all[0]
    head_ids = jnp.arange(GROUP_W) // HEAD_DIM
    bd_ones = (head_ids[:, None] == head_ids[None, :]).astype(BF16)
    bd_mean = bd_ones * (1.0 / HEAD_DIM)

    s_rows = jnp.concatenate([c, c_ctx[None, :], jnp.zeros((SUBLANES - bsz - 1, d), F32)], axis=0)
    mods = _modulation(s_rows, mod_w, mod_b)

    tm = _tile_for(n, 512)
    tmc = _tile_for(lc, 512)
    ta = _tile_for(n, 512)
    tac = _tile_for(lc, 512)
    cap_l = max(1, CAP_FACTOR * n // N_EXPERTS)
    cap_c = max(1, CAP_FACTOR * lc // N_EXPERTS)

    xl, xc = x, ctx
    for l in range(depth):
        last = l == depth - 1
        m_l = mods[l, :bsz].reshape(bsz, N_MOD, d)
        m_c = jnp.broadcast_to(mods[l, bsz].reshape(1, N_MOD, d), (bsz, N_MOD, d))
        w_in_bf = w_in[l].astype(BF16)
        w_out_bf = w_out[l].astype(BF16)
        qw = jnp.tile(d_qnorm_w[l], 4)[None, :]
        kw = jnp.tile(d_knorm_w[l], 2)[None, :]
        nw1 = norm1_w[l][None, :]
        nw2 = norm2_w[l][None, :]
        conv_w = jnp.concatenate([a_conv_w[l], jnp.zeros((SUBLANES - CONV_W, GROUP_W), F32)], axis=0)
        conv_b = a_conv_b[l][None, :]
        gate_w = jnp.stack([jnp.concatenate([_block_diag(a_gate_a_w[l, dd]), _block_diag(a_gate_x_w[l, dd])], axis=1)
                            for dd in range(2)]).astype(BF16)
        gate_b = jnp.concatenate([a_gate_a_b[l], a_gate_x_b[l]], axis=1)[:, None, :]
        lam = a_lambda[l][:, None, :]
        lb = lb_all[l][None, :]
        onorm = jnp.tile(b_onorm_w[l], B_HEADS)[None, :]
        router_t = jnp.transpose(router_w[l])
        w1 = exp_w1[l].astype(BF16)
        w3 = exp_w3[l].astype(BF16)
        w2 = exp_w2[l].astype(BF16)
        sink = c_sink[l].astype(F32)

        za_c, zb_c, qc_c, qd_c = _inproj(xc, nw1, m_c[:, 0:1], m_c[:, 1:2], w_in_bf, cos_c, sin_c, qw, kw, bd_mean, tmc)
        za_l, zb_l, qc_l, qd_l = _inproj(xl, nw1, m_l[:, 0:1], m_l[:, 1:2], w_in_bf, cos, sin, qw, kw, bd_mean, tm)

        h0 = jnp.zeros((bsz, SUBLANES, GROUP_W), F32)
        hf_c, hb_c, hlast = _lru(za_c, h0, conv_w, conv_b, gate_w, gate_b, lam, tac)
        hf_l, hb_l, _ = _lru(za_l, hlast, conv_w, conv_b, gate_w, gate_b, lam, ta)

        s0 = jnp.zeros((bsz, 2, GROUP_W, GROUP_W), F32)
        of_c, ob_c, slast = _hgrn(zb_c, s0, lb, bd_ones)
        of_l, ob_l, _ = _hgrn(zb_l, slast, lb, bd_ones)

        yc_l = _window_attention(qc_l, qc_c, sink, _tile_for(n, 256))
        yd_l = _dense_attention(qd_l, qd_c, _tile_for(n, 512), _tile_for(n, 1024), _tile_for(n, 512))

        def after_mixer(hf, hb, za, of, ob, zb, yc, yd, xin, mm, tile, cap):
            mrows = jnp.concatenate([mm[:, 2:5], jnp.zeros((bsz, SUBLANES - 3, d), F32)], axis=1)
            xmid, h2, aff_t = _outproj(hf, hb, za, of, ob, zb, yc, yd, xin, w_out_bf, mrows, nw2, onorm, bd_mean,
                                       router_t, tile)
            idx, gate = _route(aff_t, xin.shape[1], cap)
            idx = idx.reshape(bsz * N_EXPERTS, 1, cap)
            gate = gate.reshape(bsz, N_EXPERTS, cap, 1)
            return _expert_ffn(idx, gate, h2, xmid, mm[:, 5:6], w1, w3, w2)

        xl = after_mixer(hf_l, hb_l, za_l, of_l, ob_l, zb_l, yc_l, yd_l, xl, m_l, tm, cap_l)
        if not last:
            yc_c = _ctx_attention(qc_c, sink, True)
            yd_c = _ctx_attention(qd_c, sink, False)
            xc = after_mixer(hf_c, hb_c, za_c, of_c, ob_c, zb_c, yc_c, yd_c, xc, m_c, tmc, cap_c)
    return _final_norm(xl, final_norm_w[None, :], tm)
```

```python
import functools
import math

import jax
import jax.numpy as jnp
import numpy as np
from jax import lax
from jax.experimental import pallas as pl
from jax.experimental.pallas import tpu as pltpu

D_MODEL = 1024
GRID_W = 64
GROUP_W = D_MODEL // 4
HEAD_DIM = 64
EPS = 1e-6
A_BLOCKS = 4
CONV_W = 4
LRU_C = 8.0
B_HEADS = 4
WINDOW = 128
ROPE_THETA = 10000.0
N_EXPERTS = 16
CAP_FACTOR = 2
D_EXPERT = 1024
D_IN = 2816
N_MOD = 6

LANES = 128
SUBLANES = 8
VMEM_LIMIT_BYTES = 56 * 1024 * 1024

NEG_BIG = -1e30
LOG2_E = math.log2(math.e)
F32 = jnp.float32
BF16 = jnp.bfloat16


def _cparams(sem):
    return pltpu.CompilerParams(dimension_semantics=sem, vmem_limit_bytes=VMEM_LIMIT_BYTES)


def _dot(a, b):
    return jnp.dot(a, b, preferred_element_type=F32)


def _dot_nt(a, b):
    return lax.dot_general(a, b, (((1,), (1,)), ((), ())), preferred_element_type=F32)


def _silu(x):
    return x * (1.0 / (1.0 + jnp.exp(-x)))


def _sigmoid(x):
    return 1.0 / (1.0 + jnp.exp(-x))


def _iota(shape, dim):
    return lax.broadcasted_iota(jnp.int32, shape, dim)


def _mod_kernel(s_ref, w_ref, b_ref, o_ref):
    s = _silu(s_ref[...])
    o_ref[0] = jnp.dot(s, w_ref[0], precision=lax.Precision.HIGHEST, preferred_element_type=F32) + b_ref[0]


def _modulation(s_rows, mod_w, mod_b):
    depth, d, n = mod_w.shape
    tn = 1536
    return pl.pallas_call(
        _mod_kernel,
        grid=(depth, n // tn),
        in_specs=[pl.BlockSpec((SUBLANES, d), lambda l, j: (0, 0)),
                  pl.BlockSpec((1, d, tn), lambda l, j: (l, 0, j)),
                  pl.BlockSpec((1, 1, tn), lambda l, j: (l, 0, j))],
        out_specs=pl.BlockSpec((1, SUBLANES, tn), lambda l, j: (l, 0, j)),
        out_shape=jax.ShapeDtypeStruct((depth, SUBLANES, n), F32),
        compiler_params=_cparams(("parallel", "parallel")),
        name="modulation",
    )(s_rows, mod_w, mod_b.reshape(depth, 1, n))


def _rope(x, cos, sin_signed):
    w = x.shape[-1]
    lane = _iota(x.shape, 1)
    partner = jnp.where((lane % 32) < 16, pltpu.roll(x, w - 16, 1), pltpu.roll(x, 16, 1))
    return x * cos + partner * sin_signed


def _head_rms(x, bd_mean, w):
    ms = _dot((x * x).astype(BF16), bd_mean)
    return x * lax.rsqrt(ms + EPS) * w


def _pack_qkv(o_ref, q, k, v):
    o_ref[0, :, 0:256] = q.astype(BF16)
    o_ref[0, :, 256:384] = k.astype(BF16)
    o_ref[0, :, 384:512] = pltpu.roll(k, 64, 1).astype(BF16)
    o_ref[0, :, 512:640] = v.astype(BF16)
    o_ref[0, :, 640:768] = pltpu.roll(v, 64, 1).astype(BF16)


def _inproj_kernel(x_ref, nw_ref, sh_ref, sc_ref, w_ref, cos_ref, sin_ref, qw_ref, kw_ref, bd_ref,
                   za_ref, zb_ref, qc_ref, qd_ref):
    x = x_ref[0]
    y = x * lax.rsqrt(jnp.mean(x * x, axis=-1, keepdims=True) + EPS) * nw_ref[...]
    h = y * (1.0 + sc_ref[0]) + sh_ref[0]
    z = _dot(h.astype(BF16), w_ref[...])
    za_ref[0] = z[:, 0:512]
    zb_ref[0] = z[:, 512:1792]
    cos = cos_ref[...]
    sin = sin_ref[...]
    scale = HEAD_DIM ** -0.5
    cq = _rope(z[:, 1792:2048], cos, sin) * scale
    ck = _rope(z[:, 2048:2176], cos[:, 0:128], sin[:, 0:128])
    _pack_qkv(qc_ref, cq, ck, z[:, 2176:2304])
    bd = bd_ref[...]
    dq = _rope(_head_rms(z[:, 2304:2560], bd, qw_ref[...]), cos, sin) * (scale * LOG2_E)
    dk = _rope(_head_rms(z[:, 2560:2688], bd[0:128, 0:128], kw_ref[...]), cos[:, 0:128], sin[:, 0:128])
    _pack_qkv(qd_ref, dq, dk, z[:, 2688:2816])


def _inproj(x, norm_w, shift, scale, w_in_bf, cos, sin, qw, kw, bd_mean, tm):
    b, n, d = x.shape
    row = lambda bi, i: (bi, i, 0)
    const2 = lambda bi, i: (0, 0)
    return pl.pallas_call(
        _inproj_kernel,
        grid=(b, n // tm),
        in_specs=[pl.BlockSpec((1, tm, d), row),
                  pl.BlockSpec((1, d), const2),
                  pl.BlockSpec((1, 1, d), lambda bi, i: (bi, 0, 0)),
                  pl.BlockSpec((1, 1, d), lambda bi, i: (bi, 0, 0)),
                  pl.BlockSpec((d, D_IN), const2),
                  pl.BlockSpec((tm, 256), lambda bi, i: (i, 0)),
                  pl.BlockSpec((tm, 256), lambda bi, i: (i, 0)),
                  pl.BlockSpec((1, 256), const2),
                  pl.BlockSpec((1, 128), const2),
                  pl.BlockSpec((256, 256), const2)],
        out_specs=[pl.BlockSpec((1, tm, 512), row), pl.BlockSpec((1, tm, 1280), row),
                   pl.BlockSpec((1, tm, 768), row), pl.BlockSpec((1, tm, 768), row)],
        out_shape=[jax.ShapeDtypeStruct((b, n, 512), F32), jax.ShapeDtypeStruct((b, n, 1280), F32),
                   jax.ShapeDtypeStruct((b, n, 768), BF16), jax.ShapeDtypeStruct((b, n, 768), BF16)],
        compiler_params=_cparams(("parallel", "parallel")),
        name="inproj",
    )(x, norm_w, shift, scale, w_in_bf, cos, sin, qw, kw, bd_mean)


def _lru_kernel(fc_ref, fp_ref, fn_ref, bc_ref, bp_ref, bn_ref, h0_ref, cw_ref, cb_ref, gw_ref, gb_ref, lam_ref,
                hf_ref, hb_ref, hl_ref, carry_ref, *, t, nt):
    i = pl.program_id(1)

    @pl.when(i == 0)
    def _():
        carry_ref[...] = h0_ref[0]

    row = _iota((t, GROUP_W), 0)
    cw = cw_ref[...]

    def coeffs(cur_ref, prev_ref, next_ref, tile, d):
        prev = jnp.where(tile == 0, 0.0, prev_ref[0])
        nxt = jnp.where(tile == nt - 1, 0.0, next_ref[0])
        ext = jnp.concatenate([prev, cur_ref[0], nxt], axis=0)
        n_ext = t + 2 * SUBLANES
        xa = cb_ref[...] + cw[2:3] * ext[SUBLANES:SUBLANES + t]
        for j, off in ((0, 2), (1, 1), (3, -1)):
            xa = xa + cw[j:j + 1] * pltpu.roll(ext, off % n_ext, 0)[SUBLANES:SUBLANES + t]
        g = _dot(xa.astype(BF16), gw_ref[d]) + gb_ref[d]
        r = _sigmoid(g[:, 0:GROUP_W])
        gi = _sigmoid(g[:, GROUP_W:2 * GROUP_W])
        lam = lam_ref[d]
        softplus = jnp.maximum(-lam, 0.0) + jnp.log(1.0 + jnp.exp(-jnp.abs(lam)))
        log_a = (-LRU_C) * r * softplus
        a = jnp.exp(log_a)
        u = jnp.sqrt(1.0 - jnp.exp(2.0 * log_a)) * (gi * xa)
        return a, u

    a, u = coeffs(fc_ref, fp_ref, fn_ref, i, 0)
    s = 1
    while s < t:
        keep = row >= s
        a_sh = jnp.where(keep, pltpu.roll(a, s, 0), 1.0)
        u_sh = jnp.where(keep, pltpu.roll(u, s, 0), 0.0)
        u = a * u_sh + u
        a = a * a_sh
        s *= 2
    h = u + a * carry_ref[0:1]
    hf_ref[0] = h
    carry_ref[0:1] = hf_ref[0, t - 1:t, :]

    a, u = coeffs(bc_ref, bp_ref, bn_ref, nt - 1 - i, 1)
    s = 1
    while s < t:
        keep = row < t - s
        a_sh = jnp.where(keep, pltpu.roll(a, t - s, 0), 1.0)
        u_sh = jnp.where(keep, pltpu.roll(u, t - s, 0), 0.0)
        u = a * u_sh + u
        a = a * a_sh
        s *= 2
    h = u + a * carry_ref[1:2]
    hb_ref[0] = h
    carry_ref[1:2] = hb_ref[0, 0:1, :]

    @pl.when(i == nt - 1)
    def _():
        hl_ref[0] = carry_ref[...]


def _lru(za, h0, conv_w, conv_b, gate_w, gate_b, lam, t):
    b, n, _ = za.shape
    nt = n // t
    t8 = t // SUBLANES
    n8 = n // SUBLANES
    cur_f = lambda bi, i: (bi, i, 0)
    prev_f = lambda bi, i: (bi, jnp.maximum(i * t8 - 1, 0), 0)
    next_f = lambda bi, i: (bi, jnp.minimum((i + 1) * t8, n8 - 1), 0)
    cur_b = lambda bi, i: (bi, nt - 1 - i, 0)
    prev_b = lambda bi, i: (bi, jnp.maximum((nt - 1 - i) * t8 - 1, 0), 0)
    next_b = lambda bi, i: (bi, jnp.minimum((nt - i) * t8, n8 - 1), 0)
    c2 = lambda bi, i: (0, 0)
    c3 = lambda bi, i: (0, 0, 0)
    halo = (1, SUBLANES, GROUP_W)
    return pl.pallas_call(
        functools.partial(_lru_kernel, t=t, nt=nt),
        grid=(b, nt),
        in_specs=[pl.BlockSpec((1, t, GROUP_W), cur_f), pl.BlockSpec(halo, prev_f), pl.BlockSpec(halo, next_f),
                  pl.BlockSpec((1, t, GROUP_W), cur_b), pl.BlockSpec(halo, prev_b), pl.BlockSpec(halo, next_b),
                  pl.BlockSpec((1, SUBLANES, GROUP_W), lambda bi, i: (bi, 0, 0)),
                  pl.BlockSpec((SUBLANES, GROUP_W), c2), pl.BlockSpec((1, GROUP_W), c2),
                  pl.BlockSpec((2, GROUP_W, 2 * GROUP_W), c3), pl.BlockSpec((2, 1, 2 * GROUP_W), c3),
                  pl.BlockSpec((2, 1, GROUP_W), c3)],
        out_specs=[pl.BlockSpec((1, t, GROUP_W), cur_f), pl.BlockSpec((1, t, GROUP_W), cur_b),
                   pl.BlockSpec((1, SUBLANES, GROUP_W), lambda bi, i: (bi, 0, 0))],
        out_shape=[jax.ShapeDtypeStruct((b, n, GROUP_W), F32), jax.ShapeDtypeStruct((b, n, GROUP_W), F32),
                   jax.ShapeDtypeStruct((b, SUBLANES, GROUP_W), F32)],
        scratch_shapes=[pltpu.VMEM((SUBLANES, GROUP_W), F32)],
        compiler_params=_cparams(("parallel", "arbitrary")),
        name="rglru",
    )(za, za, za, za, za, za, h0, conv_w, conv_b, gate_w, gate_b, lam)


B_CHUNK = 128
B_SUB = 32


def _hgrn_chunk(q_raw, f_raw, v, lb, st_ref, d, b_ref, k_ref, v_ref, bd_ones, reverse):
    t = B_CHUNK
    n_sub = t // B_SUB
    q = _silu(q_raw) * (GROUP_W // B_HEADS) ** -0.5
    log_sig = jnp.minimum(f_raw, 0.0) - jnp.log(1.0 + jnp.exp(-jnp.abs(f_raw)))
    log_lb = jnp.log(lb)
    y = jnp.log(1.0 - lb) + log_sig
    log_f = jnp.maximum(log_lb, y) + jnp.log(1.0 + jnp.exp(-jnp.abs(log_lb - y)))
    k = (1.0 - lb) * _sigmoid(-f_raw)
    row = _iota((t, GROUP_W), 0)
    b = log_f
    s = 1
    while s < t:
        if reverse:
            b = b + jnp.where(row < t - s, pltpu.roll(b, t - s, 0), 0.0)
        else:
            b = b + jnp.where(row >= s, pltpu.roll(b, s, 0), 0.0)
        s *= 2
    b_ref[...] = b
    k_ref[...] = k
    v_ref[...] = v
    edge = 0 if reverse else t - 1
    b_edge = b_ref[edge:edge + 1, :]
    st = st_ref[d]
    lane = _iota((B_SUB, GROUP_W), 1)
    head_masks = [(lane >= 64 * hh) & (lane < 64 * (hh + 1)) for hh in range(B_HEADS)]
    trow = _iota((B_SUB, GROUP_W), 0)
    srow = _iota((t, GROUP_W), 0)

    o_inter = _dot_nt((q * jnp.exp(b)).astype(BF16), st.astype(BF16))
    outs = []
    for i in range(n_sub):
        lo = i * B_SUB
        qb = q[lo:lo + B_SUB]
        bb = b[lo:lo + B_SUB]
        acc = o_inter[lo:lo + B_SUB]
        has_off = (i < n_sub - 1) if reverse else (i > 0)
        if has_off:
            ref_row = lo + B_SUB if reverse else lo - 1
            r_i = b_ref[ref_row:ref_row + 1, :]
            qh = qb * jnp.exp(bb - r_i)
            key_ok = (srow >= lo + B_SUB) if reverse else (srow < lo)
            kh = jnp.where(key_ok, k * jnp.exp(jnp.minimum(r_i - b, 0.0)), 0.0)
            qstack = jnp.concatenate([jnp.where(m, qh, 0.0) for m in head_masks], axis=0).astype(BF16)
            att = _dot_nt(qstack, kh.astype(BF16))
            res = _dot(att.astype(BF16), v.astype(BF16))
            for hh in range(B_HEADS):
                acc = acc + jnp.where(head_masks[hh], res[hh * B_SUB:(hh + 1) * B_SUB], 0.0)
        pieces = []
        for sl in range(B_SUB):
            srow_b = b_ref[lo + sl:lo + sl + 1, :]
            srow_k = k_ref[lo + sl:lo + sl + 1, :]
            ok = (trow <= sl) if reverse else (trow >= sl)
            e = jnp.exp(jnp.where(ok, bb - srow_b, NEG_BIG)) * (qb * srow_k)
            pieces.append(e.astype(BF16))
        g = _dot(jnp.concatenate(pieces, axis=0), bd_ones)
        for sl in range(B_SUB):
            acc = acc + g[sl * B_SUB:(sl + 1) * B_SUB] * v_ref[lo + sl:lo + sl + 1, :]
        outs.append(acc)
    kt = k * jnp.exp(b_edge - b)
    upd = _dot(jnp.transpose(v).astype(BF16), kt.astype(BF16))
    vrow = _iota((GROUP_W, GROUP_W), 0) // 64
    kcol = _iota((GROUP_W, GROUP_W), 1) // 64
    st_ref[d] = st * jnp.exp(b_edge) + jnp.where(vrow == kcol, upd, 0.0)
    return jnp.concatenate(outs, axis=0)


def _hgrn_kernel(qf_ref, ff_ref, vf_ref, qb_ref, fb_ref, vb_ref, s0_ref, lb_ref, bd_ref,
                 of_ref, ob_ref, sl_ref, st_ref, b_ref, k_ref, v_ref, *, nt):
    i = pl.program_id(1)

    @pl.when(i == 0)
    def _():
        st_ref[...] = s0_ref[0]

    lb = lb_ref[...]
    bd = bd_ref[...]
    of_ref[0] = _hgrn_chunk(qf_ref[0], ff_ref[0], vf_ref[0], lb, st_ref, 0, b_ref, k_ref, v_ref, bd, False)
    ob_ref[0] = _hgrn_chunk(qb_ref[0], fb_ref[0], vb_ref[0], lb, st_ref, 1, b_ref, k_ref, v_ref, bd, True)

    @pl.when(i == nt - 1)
    def _():
        sl_ref[0] = st_ref[...]


def _hgrn(zb, s0, lb, bd_ones):
    b, n, _ = zb.shape
    t = B_CHUNK
    nt = n // t
    blk = (1, t, GROUP_W)
    fwd = lambda col: (lambda bi, i: (bi, i, col))
    bwd = lambda col: (lambda bi, i: (bi, nt - 1 - i, col))
    state_spec = pl.BlockSpec((1, 2, GROUP_W, GROUP_W), lambda bi, i: (bi, 0, 0, 0))
    return pl.pallas_call(
        functools.partial(_hgrn_kernel, nt=nt),
        grid=(b, nt),
        in_specs=[pl.BlockSpec(blk, fwd(0)), pl.BlockSpec(blk, fwd(1)), pl.BlockSpec(blk, fwd(3)),
                  pl.BlockSpec(blk, bwd(0)), pl.BlockSpec(blk, bwd(2)), pl.BlockSpec(blk, bwd(3)),
                  state_spec,
                  pl.BlockSpec((1, GROUP_W), lambda bi, i: (0, 0)),
                  pl.BlockSpec((GROUP_W, GROUP_W), lambda bi, i: (0, 0))],
        out_specs=[pl.BlockSpec(blk, fwd(0)), pl.BlockSpec(blk, bwd(0)), state_spec],
        out_shape=[jax.ShapeDtypeStruct((b, n, GROUP_W), F32), jax.ShapeDtypeStruct((b, n, GROUP_W), F32),
                   jax.ShapeDtypeStruct((b, 2, GROUP_W, GROUP_W), F32)],
        scratch_shapes=[pltpu.VMEM((2, GROUP_W, GROUP_W), F32), pltpu.VMEM((t, GROUP_W), F32),
                        pltpu.VMEM((t, GROUP_W), F32), pltpu.VMEM((t, GROUP_W), F32)],
        compiler_params=_cparams(("parallel", "arbitrary")),
        name="hgrn2",
    )(zb, zb, zb, zb, zb, zb, s0, lb, bd_ones)


def _stack_q(q):
    tq = q.shape[0]
    lane = _iota((tq, LANES), 1)
    lo = lane < 64
    q0 = q[:, 0:128]
    q1 = q[:, 128:256]
    zero = jnp.zeros_like(q0)
    s1 = jnp.concatenate([jnp.where(lo, q0, zero), jnp.where(lo, zero, q1)], axis=0)
    s2 = jnp.concatenate([jnp.where(lo, zero, q0), jnp.where(lo, q1, zero)], axis=0)
    return s1, s2


def _unstack_out(r1, r2, tq):
    lane = _iota((tq, LANES), 1)
    lo = lane < 64
    o0 = jnp.where(lo, r1[0:tq], r2[0:tq])
    o1 = jnp.where(lo, r2[tq:2 * tq], r1[tq:2 * tq])
    return jnp.concatenate([o0, o1], axis=1)


def _sink_cols(sink_ref, tq):
    row = _iota((2 * tq, 1), 0)
    c1 = jnp.where(row < tq, sink_ref[0], sink_ref[3])
    c2 = jnp.where(row < tq, sink_ref[1], sink_ref[2])
    return c1, c2


def _window_kernel(sink_ref, q_ref, kp_ref, kc_ref, kn_ref, vp_ref, vc_ref, vn_ref, kx_ref, vx_ref, o_ref,
                   *, tq, n):
    i = pl.program_id(1)
    s1, s2 = _stack_q(q_ref[0])
    kband = jnp.concatenate([kp_ref[0], kc_ref[0], kn_ref[0]], axis=0)
    vband = jnp.concatenate([vp_ref[0], vc_ref[0], vn_ref[0]], axis=0)
    nband = tq + 2 * WINDOW
    qpos = i * tq + (_iota((2 * tq, nband), 0) % tq)
    kpos = i * tq - WINDOW + _iota((2 * tq, nband), 1)
    ok = (kpos >= 0) & (kpos < n) & (jnp.abs(qpos - kpos) <= WINDOW)
    sinks = _sink_cols(sink_ref, tq)
    res = []
    for qs, half, sink in ((s1, 0, sinks[0]), (s2, 1, sinks[1])):
        kx = kx_ref[0][:, half * 128:(half + 1) * 128]
        vx = vx_ref[0][:, half * 128:(half + 1) * 128]
        sc = _dot_nt(qs, kx)
        sw = jnp.where(ok, _dot_nt(qs, kband[:, half * 128:(half + 1) * 128]), NEG_BIG)
        m = jnp.maximum(jnp.maximum(jnp.max(sc, axis=-1, keepdims=True), jnp.max(sw, axis=-1, keepdims=True)), sink)
        pc = jnp.exp(sc - m)
        pw = jnp.exp(sw - m)
        den = jnp.sum(pc, axis=-1, keepdims=True) + jnp.sum(pw, axis=-1, keepdims=True) + jnp.exp(sink - m)
        num = _dot(pc.astype(BF16), vx) + _dot(pw.astype(BF16), vband[:, half * 128:(half + 1) * 128])
        res.append(num / den)
    o_ref[0] = _unstack_out(res[0], res[1], tq).astype(BF16)


def _window_attention(qkv, qkv_ctx, sink, tq):
    b, n, _ = qkv.shape
    lc = qkv_ctx.shape[1]
    w = WINDOW
    r = tq // w
    nw = n // w
    prev = lambda col: (lambda bi, i, s: (bi, jnp.maximum(i * r - 1, 0), col))
    cur = lambda col: (lambda bi, i, s: (bi, i, col))
    nxt = lambda col: (lambda bi, i, s: (bi, jnp.minimum((i + 1) * r, nw - 1), col))
    ctx = lambda col: (lambda bi, i, s: (bi, 0, col))
    grid_spec = pltpu.PrefetchScalarGridSpec(
        num_scalar_prefetch=1,
        grid=(b, n // tq),
        in_specs=[pl.BlockSpec((1, tq, 256), cur(0)),
                  pl.BlockSpec((1, w, 256), prev(1)), pl.BlockSpec((1, tq, 256), cur(1)),
                  pl.BlockSpec((1, w, 256), nxt(1)),
                  pl.BlockSpec((1, w, 256), prev(2)), pl.BlockSpec((1, tq, 256), cur(2)),
                  pl.BlockSpec((1, w, 256), nxt(2)),
                  pl.BlockSpec((1, lc, 256), ctx(1)), pl.BlockSpec((1, lc, 256), ctx(2))],
        out_specs=pl.BlockSpec((1, tq, 256), cur(0)),
    )
    return pl.pallas_call(
        functools.partial(_window_kernel, tq=tq, n=n),
        grid_spec=grid_spec,
        out_shape=jax.ShapeDtypeStruct((b, n, 256), BF16),
        compiler_params=_cparams(("parallel", "parallel")),
        name="window_attn",
    )(sink, qkv, qkv, qkv, qkv, qkv, qkv, qkv, qkv_ctx, qkv_ctx)


def _ctx_attn_kernel(sink_ref, q_ref, k_ref, v_ref, o_ref, *, tq, use_sink):
    s1, s2 = _stack_q(q_ref[0])
    sinks = _sink_cols(sink_ref, tq)
    res = []
    for qs, half, sink in ((s1, 0, sinks[0]), (s2, 1, sinks[1])):
        kx = k_ref[0][:, half * 128:(half + 1) * 128]
        vx = v_ref[0][:, half * 128:(half + 1) * 128]
        sc = _dot_nt(qs, kx)
        m = jnp.max(sc, axis=-1, keepdims=True)
        if use_sink:
            m = jnp.maximum(m, sink)
        p = jnp.exp(sc - m) if use_sink else jnp.exp2(sc - m)
        den = jnp.sum(p, axis=-1, keepdims=True)
        if use_sink:
            den = den + jnp.exp(sink - m)
        res.append(_dot(p.astype(BF16), vx) / den)
    o_ref[0] = _unstack_out(res[0], res[1], tq).astype(BF16)


def _ctx_attention(qkv_ctx, sink, use_sink):
    b, lc, _ = qkv_ctx.shape
    col = lambda c: (lambda bi, s: (bi, 0, c))
    grid_spec = pltpu.PrefetchScalarGridSpec(
        num_scalar_prefetch=1,
        grid=(b,),
        in_specs=[pl.BlockSpec((1, lc, 256), col(0)), pl.BlockSpec((1, lc, 256), col(1)),
                  pl.BlockSpec((1, lc, 256), col(2))],
        out_specs=pl.BlockSpec((1, lc, 256), col(0)),
    )
    return pl.pallas_call(
        functools.partial(_ctx_attn_kernel, tq=lc, use_sink=use_sink),
        grid_spec=grid_spec,
        out_shape=jax.ShapeDtypeStruct((b, lc, 256), BF16),
        compiler_params=_cparams(("parallel",)),
        name="ctx_attn",
    )(sink, qkv_ctx, qkv_ctx, qkv_ctx)


def _dense_kernel(q_ref, k_ref, v_ref, kx_ref, vx_ref, o_ref, qs_ref, m_ref, acc_ref, *, tq, tk, ck, rb, nk):
    ki = pl.program_id(2)
    rows = 2 * tq

    def update(half, kc, vc):
        n = kc.shape[0]
        vext = jnp.concatenate([vc, jnp.ones((n, LANES), BF16)], axis=1)
        for r in range(rows // rb):
            sl = slice(r * rb, (r + 1) * rb)
            s = _dot_nt(qs_ref[half, sl, :], kc)
            m_old = m_ref[half, sl, :]
            m_new = jnp.maximum(m_old, jnp.max(s, axis=-1, keepdims=True))
            alpha = jnp.exp2(m_old - m_new)
            p = jnp.exp2(s - jnp.tile(m_new, (1, n // LANES))).astype(BF16)
            m_ref[half, sl, :] = m_new
            acc_ref[half, sl, :] = acc_ref[half, sl, :] * jnp.tile(alpha, (1, 2)) + _dot(p, vext)

    @pl.when(ki == 0)
    def _():
        s1, s2 = _stack_q(q_ref[0])
        qs_ref[0] = s1
        qs_ref[1] = s2
        m_ref[...] = jnp.full(m_ref.shape, NEG_BIG, F32)
        acc_ref[...] = jnp.zeros(acc_ref.shape, F32)
        for half in range(2):
            update(half, kx_ref[0][:, half * 128:(half + 1) * 128], vx_ref[0][:, half * 128:(half + 1) * 128])

    for c in range(tk // ck):
        for half in range(2):
            update(half, k_ref[0, c * ck:(c + 1) * ck, half * 128:(half + 1) * 128],
                   v_ref[0, c * ck:(c + 1) * ck, half * 128:(half + 1) * 128])

    @pl.when(ki == nk - 1)
    def _():
        a1 = acc_ref[0]
        a2 = acc_ref[1]
        r1 = a1[:, 0:LANES] / a1[:, LANES:2 * LANES]
        r2 = a2[:, 0:LANES] / a2[:, LANES:2 * LANES]
        o_ref[0] = _unstack_out(r1, r2, tq).astype(BF16)


def _dense_attention(qkv, qkv_ctx, tq, tk, ck):
    b, n, _ = qkv.shape
    lc = qkv_ctx.shape[1]
    nk = n // tk
    rb = min(2 * tq, 256)
    return pl.pallas_call(
        functools.partial(_dense_kernel, tq=tq, tk=tk, ck=ck, rb=rb, nk=nk),
        grid=(b, n // tq, nk),
        in_specs=[pl.BlockSpec((1, tq, 256), lambda bi, qi, ki: (bi, qi, 0)),
                  pl.BlockSpec((1, tk, 256), lambda bi, qi, ki: (bi, ki, 1)),
                  pl.BlockSpec((1, tk, 256), lambda bi, qi, ki: (bi, ki, 2)),
                  pl.BlockSpec((1, lc, 256), lambda bi, qi, ki: (bi, 0, 1)),
                  pl.BlockSpec((1, lc, 256), lambda bi, qi, ki: (bi, 0, 2))],
        out_specs=pl.BlockSpec((1, tq, 256), lambda bi, qi, ki: (bi, qi, 0)),
        out_shape=jax.ShapeDtypeStruct((b, n, 256), BF16),
        scratch_shapes=[pltpu.VMEM((2, 2 * tq, LANES), BF16), pltpu.VMEM((2, 2 * tq, LANES), F32),
                        pltpu.VMEM((2, 2 * tq, 2 * LANES), F32)],
        compiler_params=_cparams(("parallel", "parallel", "arbitrary")),
        name="dense_attn",
    )(qkv, qkv, qkv, qkv_ctx, qkv_ctx)


def _gelu_tanh(x):
    return 0.5 * x * (1.0 + jnp.tanh(math.sqrt(2.0 / math.pi) * (x + 0.044715 * (x * x * x))))


def _outproj_kernel(hf_ref, hb_ref, ay_ref, of_ref, ob_ref, bg_ref, yc_ref, yd_ref, x_ref, w_ref, m_ref, nw_ref,
                    ow_ref, bd_ref, rw_ref, xo_ref, h2_ref, aff_ref):
    ya = (hf_ref[0] + hb_ref[0]) * _gelu_tanh(ay_ref[0])
    o = of_ref[0] + ob_ref[0]
    ms = _dot((o * o).astype(BF16), bd_ref[...])
    yb = o * lax.rsqrt(ms + EPS) * ow_ref[...] * _silu(bg_ref[0])
    y = jnp.concatenate([ya.astype(BF16), yb.astype(BF16), yc_ref[0], yd_ref[0]], axis=1)
    out = _dot(y, w_ref[...])
    m = m_ref[0]
    x = x_ref[0] + m[0:1] * out
    xo_ref[0] = x
    hn = x * lax.rsqrt(jnp.mean(x * x, axis=-1, keepdims=True) + EPS) * nw_ref[...]
    h2 = hn * (1.0 + m[2:3]) + m[1:2]
    h2_ref[0] = h2
    logits = lax.dot_general(rw_ref[...], h2, (((1,), (1,)), ((), ())), precision=lax.Precision.HIGHEST,
                             preferred_element_type=F32)
    mx = jnp.max(logits, axis=0, keepdims=True)
    e = jnp.exp(logits - mx)
    aff_ref[0] = e / jnp.sum(e, axis=0, keepdims=True)


def _outproj(hf, hb, za, of, ob, zb, yc, yd, x, w_out_bf, mrows, norm2_w, onorm_w, bd_mean, router_t, tm):
    b, n, d = x.shape
    row = lambda col: (lambda bi, i: (bi, i, col))
    c2 = lambda bi, i: (0, 0)
    g = (1, tm, GROUP_W)
    return pl.pallas_call(
        _outproj_kernel,
        grid=(b, n // tm),
        in_specs=[pl.BlockSpec(g, row(0)), pl.BlockSpec(g, row(0)), pl.BlockSpec(g, row(1)),
                  pl.BlockSpec(g, row(0)), pl.BlockSpec(g, row(0)), pl.BlockSpec(g, row(4)),
                  pl.BlockSpec(g, row(0)), pl.BlockSpec(g, row(0)),
                  pl.BlockSpec((1, tm, d), row(0)),
                  pl.BlockSpec((d, d), c2),
                  pl.BlockSpec((1, SUBLANES, d), lambda bi, i: (bi, 0, 0)),
                  pl.BlockSpec((1, d), c2), pl.BlockSpec((1, GROUP_W), c2),
                  pl.BlockSpec((GROUP_W, GROUP_W), c2), pl.BlockSpec((N_EXPERTS, d), c2)],
        out_specs=[pl.BlockSpec((1, tm, d), row(0)), pl.BlockSpec((1, tm, d), row(0)),
                   pl.BlockSpec((1, N_EXPERTS, tm), lambda bi, i: (bi, 0, i))],
        out_shape=[jax.ShapeDtypeStruct((b, n, d), F32), jax.ShapeDtypeStruct((b, n, d), F32),
                   jax.ShapeDtypeStruct((b, N_EXPERTS, n), F32)],
        compiler_params=_cparams(("parallel", "parallel")),
        name="outproj",
    )(hf, hb, za, of, ob, zb, yc, yd, x, w_out_bf, mrows, norm2_w, onorm_w, bd_mean, router_t)


def _topk_kernel(aff_ref, idx_ref, gate_ref, *, gp, cap):
    x_all = aff_ref[0]
    xb_all = pltpu.bitcast(x_all, jnp.int32)
    kf = float(cap)

    lo = jnp.zeros((N_EXPERTS, 1, 1), jnp.int32)
    hi = jnp.full((N_EXPERTS, 1, 1), 0x7F800000, jnp.int32)

    def bisect(_, carry):
        lo, hi = carry
        mid = lo + lax.shift_right_logical(hi - lo, 1)
        cnt = jnp.sum(jnp.sum(jnp.where(xb_all >= mid, 1.0, 0.0), axis=2, keepdims=True), axis=1, keepdims=True)
        ge = cnt >= kf
        return jnp.where(ge, mid, lo), jnp.where(ge, hi, mid)

    lo, hi = lax.fori_loop(0, 31, bisect, (lo, hi))

    upper = jnp.where(_iota((LANES, LANES), 0) <= _iota((LANES, LANES), 1), 1.0, 0.0).astype(BF16)
    strict = jnp.where(_iota((gp, gp), 1) < _iota((gp, gp), 0), 1.0, 0.0).astype(BF16)
    g_col = _iota((gp, 1), 0).astype(F32)
    j_col = _iota((LANES, 1), 0).astype(F32)
    slot = _iota((1, cap), 1).astype(F32)

    def prefix(mask_f):
        within = _dot(mask_f.astype(BF16), upper)
        before = jnp.sum(_dot(strict, mask_f.astype(BF16)), axis=1, keepdims=True)
        return within, before

    for e in range(N_EXPERTS):
        x = x_all[e]
        xb = xb_all[e]
        thr = lo[e]
        gt = jnp.where(xb > thr, 1.0, 0.0)
        eq = jnp.where(xb == thr, 1.0, 0.0)
        need = kf - jnp.sum(jnp.sum(gt, axis=1, keepdims=True), axis=0, keepdims=True)
        eq_within, eq_before = prefix(eq)
        tie_rank = eq_before + eq_within - eq
        sel = jnp.maximum(gt, jnp.where(tie_rank < need, eq, 0.0))
        within, before = prefix(sel)
        incl = before + within[:, LANES - 1:LANES]
        g_of_s = jnp.sum(jnp.where(incl <= slot, 1.0, 0.0), axis=0, keepdims=True)
        base = jnp.max(jnp.where(before <= slot, before, 0.0), axis=0, keepdims=True)
        rank = slot - base
        onehot = jnp.where(g_col == g_of_s, 1.0, 0.0).astype(BF16)
        counts = _dot(jnp.transpose(within).astype(BF16), onehot)
        j_of_s = jnp.sum(jnp.where(counts <= rank, 1.0, 0.0), axis=0, keepdims=True)
        idx_ref[0, e] = (g_of_s * float(LANES) + j_of_s).astype(jnp.int32)
        xt = jnp.transpose(x)
        x_hi = xt.astype(BF16)
        r1 = xt - x_hi.astype(F32)
        x_mid = r1.astype(BF16)
        x_lo = (r1 - x_mid.astype(F32)).astype(BF16)
        vals = _dot(x_hi, onehot) + _dot(x_mid, onehot) + _dot(x_lo, onehot)
        gate_ref[0, e] = jnp.sum(jnp.where(j_col == j_of_s, vals, 0.0), axis=0, keepdims=True)


def _route(aff_t, n_tokens, cap):
    b = aff_t.shape[0]
    g = n_tokens // LANES
    gp = -(-g // LANES) * LANES
    aff4 = aff_t.reshape(b, N_EXPERTS, g, LANES)
    if gp != g:
        aff4 = jnp.pad(aff4, ((0, 0), (0, 0), (0, gp - g), (0, 0)), constant_values=-1.0)
    return pl.pallas_call(
        functools.partial(_topk_kernel, gp=gp, cap=cap),
        grid=(b,),
        in_specs=[pl.BlockSpec((1, N_EXPERTS, gp, LANES), lambda bi: (bi, 0, 0, 0))],
        out_specs=[pl.BlockSpec((1, N_EXPERTS, 1, cap), lambda bi: (bi, 0, 0, 0)),
                   pl.BlockSpec((1, N_EXPERTS, 1, cap), lambda bi: (bi, 0, 0, 0))],
        out_shape=[jax.ShapeDtypeStruct((b, N_EXPERTS, 1, cap), jnp.int32),
                   jax.ShapeDtypeStruct((b, N_EXPERTS, 1, cap), F32)],
        compiler_params=_cparams(("parallel",)),
        name="route_topk",
    )(aff4)


def _ffn_kernel(idx_ref, h_hbm, acc_in, gate_ref, m_ref, w1_ref, w3_ref, w2_ref, acc_out,
                xbuf, abuf, sem_x, sem_a, sem_o, *, cap, rows):
    b = pl.program_id(1)
    del acc_in
    n_chunks = cap // rows

    def gather(c, slot):
        def body(r, _):
            tok = idx_ref[0, 0, c * rows + r]
            pltpu.make_async_copy(h_hbm.at[b, pl.ds(tok, 1), :], xbuf.at[slot, pl.ds(r, 1), :],
                                  sem_x.at[slot]).start()
            pltpu.make_async_copy(acc_out.at[b, pl.ds(tok, 1), :], abuf.at[slot, pl.ds(r, 1), :],
                                  sem_a.at[slot]).start()
            return 0

        lax.fori_loop(0, rows, body, 0, unroll=8)

    def wait_rows(buf, sem, slot):
        pltpu.make_async_copy(h_hbm.at[b, pl.ds(0, rows), :], buf.at[slot], sem.at[slot]).wait()

    def scatter(c, slot):
        def body(r, _):
            tok = idx_ref[0, 0, c * rows + r]
            pltpu.make_async_copy(abuf.at[slot, pl.ds(r, 1), :], acc_out.at[b, pl.ds(tok, 1), :],
                                  sem_o.at[slot]).start()
            return 0

        lax.fori_loop(0, rows, body, 0, unroll=8)

    def wait_scatter(slot):
        pltpu.make_async_copy(abuf.at[slot], acc_out.at[b, pl.ds(0, rows), :], sem_o.at[slot]).wait()

    gather(0, 0)
    for c in range(n_chunks):
        slot = c % 2
        if c + 1 < n_chunks:
            if c >= 1:
                wait_scatter(1 - slot)
            gather(c + 1, 1 - slot)
        wait_rows(xbuf, sem_x, slot)
        xb = xbuf[slot].astype(BF16)
        hid = _silu(_dot(xb, w1_ref[0])) * _dot(xb, w3_ref[0])
        y = _dot(hid.astype(BF16), w2_ref[0])
        y = y * gate_ref[0, 0, c * rows:(c + 1) * rows, :] * m_ref[0]
        wait_rows(abuf, sem_a, slot)
        abuf[slot] = abuf[slot] + y
        scatter(c, slot)
    if n_chunks >= 2:
        wait_scatter(n_chunks % 2)
    wait_scatter((n_chunks - 1) % 2)


def _expert_ffn(idx, gate, h2, acc, mgate, w1, w3, w2):
    b, n, d = h2.shape
    cap = idx.shape[-1]
    rows = min(cap, 256)
    return pl.pallas_call(
        functools.partial(_ffn_kernel, cap=cap, rows=rows),
        grid=(N_EXPERTS, b),
        in_specs=[pl.BlockSpec((1, 1, cap), lambda e, bi: (bi * N_EXPERTS + e, 0, 0), memory_space=pltpu.SMEM),
                  pl.BlockSpec(memory_space=pl.ANY),
                  pl.BlockSpec(memory_space=pl.ANY),
                  pl.BlockSpec((1, 1, cap, 1), lambda e, bi: (bi, e, 0, 0)),
                  pl.BlockSpec((1, 1, d), lambda e, bi: (bi, 0, 0)),
                  pl.BlockSpec((1, d, D_EXPERT), lambda e, bi: (e, 0, 0)),
                  pl.BlockSpec((1, d, D_EXPERT), lambda e, bi: (e, 0, 0)),
                  pl.BlockSpec((1, D_EXPERT, d), lambda e, bi: (e, 0, 0))],
        out_specs=pl.BlockSpec(memory_space=pl.ANY),
        out_shape=jax.ShapeDtypeStruct((b, n, d), F32),
        scratch_shapes=[pltpu.VMEM((2, rows, d), F32), pltpu.VMEM((2, rows, d), F32),
                        pltpu.SemaphoreType.DMA((2,)), pltpu.SemaphoreType.DMA((2,)),
                        pltpu.SemaphoreType.DMA((2,))],
        input_output_aliases={2: 0},
        compiler_params=pltpu.CompilerParams(dimension_semantics=("arbitrary", "arbitrary"),
                                             vmem_limit_bytes=VMEM_LIMIT_BYTES),
        name="expert_ffn",
    )(idx, h2, acc, gate, mgate, w1, w3, w2)


def _final_norm_kernel(x_ref, w_ref, o_ref):
    x = x_ref[0]
    o_ref[0] = x * lax.rsqrt(jnp.mean(x * x, axis=-1, keepdims=True) + EPS) * w_ref[...]


def _final_norm(x, w, tm):
    b, n, d = x.shape
    return pl.pallas_call(
        _final_norm_kernel,
        grid=(b, n // tm),
        in_specs=[pl.BlockSpec((1, tm, d), lambda bi, i: (bi, i, 0)), pl.BlockSpec((1, d), lambda bi, i: (0, 0))],
        out_specs=pl.BlockSpec((1, tm, d), lambda bi, i: (bi, i, 0)),
        out_shape=jax.ShapeDtypeStruct((b, n, d), F32),
        compiler_params=_cparams(("parallel", "parallel")),
        name="final_norm",
    )(x, w)


def _rope_tables(n):
    half = HEAD_DIM // 2
    n_freq = half // 2
    inv = ROPE_THETA ** (-jnp.arange(n_freq, dtype=F32) * 2.0 / half)
    rows = n // GRID_W
    r = jnp.repeat(jnp.arange(rows, dtype=F32), GRID_W)
    col = jnp.tile(jnp.arange(GRID_W, dtype=F32), rows)
    ang = jnp.stack([r[:, None] * inv, col[:, None] * inv], axis=1)
    cos = jnp.cos(ang)[:, :, None, :]
    sin = jnp.sin(ang)[:, :, None, :]
    cos_h = jnp.broadcast_to(cos, (n, 2, 2, n_freq)).reshape(n, HEAD_DIM)
    sin_h = (sin * jnp.array([-1.0, 1.0], F32)[None, None, :, None]).reshape(n, HEAD_DIM)
    return jnp.tile(cos_h, (1, 4)), jnp.tile(sin_h, (1, 4))


def _block_diag(w):
    nb, bi, bj = w.shape
    eye = jnp.eye(nb, dtype=w.dtype)
    return (eye[:, None, :, None] * w[:, :, None, :]).reshape(nb * bi, nb * bj)


def _tile_for(n, pref):
    t = pref
    while n % t:
        t //= 2
    return t


def kernel(x, c, ctx, c_ctx, mod_w, mod_b, norm1_w, w_in, a_conv_w, a_conv_b, a_gate_a_w, a_gate_a_b, a_gate_x_w,
           a_gate_x_b, a_lambda, b_lb_logits, b_onorm_w, c_sink, d_qnorm_w, d_knorm_w, w_out, norm2_w, router_w,
           exp_w1, exp_w3, exp_w2, final_norm_w):
    bsz, n, d = x.shape
    lc = ctx.shape[1]
    depth = mod_w.shape[0]
    cos, sin = _rope_tables(n)
    cos_c = jnp.ones((lc, 256), F32)
    sin_c = jnp.zeros((lc, 256), F32)
    lb_all = jnp.cumsum(jax.nn.softmax(b_lb_logits.astype(F32), axis=0), axis=0)
    lb_all = lb_all - lb_all[0]
    head_ids = jnp.arange(GROUP_W) // HEAD_DIM
    bd_ones = (head_ids[:, None] == head_ids[None, :]).astype(BF16)
    bd_mean = bd_ones * (1.0 / HEAD_DIM)

    s_rows = jnp.concatenate([c, c_ctx[None, :], jnp.zeros((SUBLANES - bsz - 1, d), F32)], axis=0)
    mods = _modulation(s_rows, mod_w, mod_b)

    tm = _tile_for(n, 512)
    tmc = _tile_for(lc, 512)
    ta = _tile_for(n, 512)
    tac = _tile_for(lc, 512)
    cap_l = max(1, CAP_FACTOR * n // N_EXPERTS)
    cap_c = max(1, CAP_FACTOR * lc // N_EXPERTS)

    xl, xc = x, ctx
    for l in range(depth):
        last = l == depth - 1
        m_l = mods[l, :bsz].reshape(bsz, N_MOD, d)
        m_c = jnp.broadcast_to(mods[l, bsz].reshape(1, N_MOD, d), (bsz, N_MOD, d))
        w_in_bf = w_in[l].astype(BF16)
        w_out_bf = w_out[l].astype(BF16)
        qw = jnp.tile(d_qnorm_w[l], 4)[None, :]
        kw = jnp.tile(d_knorm_w[l], 2)[None, :]
        nw1 = norm1_w[l][None, :]
        nw2 = norm2_w[l][None, :]
        conv_w = jnp.concatenate([a_conv_w[l], jnp.zeros((SUBLANES - CONV_W, GROUP_W), F32)], axis=0)
        conv_b = a_conv_b[l][None, :]
        gate_w = jnp.stack([jnp.concatenate([_block_diag(a_gate_a_w[l, dd]), _block_diag(a_gate_x_w[l, dd])], axis=1)
                            for dd in range(2)]).astype(BF16)
        gate_b = jnp.concatenate([a_gate_a_b[l], a_gate_x_b[l]], axis=1)[:, None, :]
        lam = a_lambda[l][:, None, :]
        lb = lb_all[l][None, :]
        onorm = jnp.tile(b_onorm_w[l], B_HEADS)[None, :]
        router_t = jnp.transpose(router_w[l])
        w1 = exp_w1[l].astype(BF16)
        w3 = exp_w3[l].astype(BF16)
        w2 = exp_w2[l].astype(BF16)
        sink = c_sink[l].astype(F32)

        za_c, zb_c, qc_c, qd_c = _inproj(xc, nw1, m_c[:, 0:1], m_c[:, 1:2], w_in_bf, cos_c, sin_c, qw, kw, bd_mean, tmc)
        za_l, zb_l, qc_l, qd_l = _inproj(xl, nw1, m_l[:, 0:1], m_l[:, 1:2], w_in_bf, cos, sin, qw, kw, bd_mean, tm)

        h0 = jnp.zeros((bsz, SUBLANES, GROUP_W), F32)
        hf_c, hb_c, hlast = _lru(za_c, h0, conv_w, conv_b, gate_w, gate_b, lam, tac)
        hf_l, hb_l, _ = _lru(za_l, hlast, conv_w, conv_b, gate_w, gate_b, lam, ta)

        s0 = jnp.zeros((bsz, 2, GROUP_W, GROUP_W), F32)
        of_c, ob_c, slast = _hgrn(zb_c, s0, lb, bd_ones)
        of_l, ob_l, _ = _hgrn(zb_l, slast, lb, bd_ones)

        yc_l = _window_attention(qc_l, qc_c, sink, _tile_for(n, 256))
        yd_l = _dense_attention(qd_l, qd_c, _tile_for(n, 512), _tile_for(n, 2048), 256)

        def after_mixer(hf, hb, za, of, ob, zb, yc, yd, xin, mm, tile, cap):
            mrows = jnp.concatenate([mm[:, 2:5], jnp.zeros((bsz, SUBLANES - 3, d), F32)], axis=1)
            xmid, h2, aff_t = _outproj(hf, hb, za, of, ob, zb, yc, yd, xin, w_out_bf, mrows, nw2, onorm, bd_mean,
                                       router_t, tile)
            idx, gate = _route(aff_t, xin.shape[1], cap)
            idx = idx.reshape(bsz * N_EXPERTS, 1, cap)
            gate = gate.reshape(bsz, N_EXPERTS, cap, 1)
            return _expert_ffn(idx, gate, h2, xmid, mm[:, 5:6], w1, w3, w2)

        xl = after_mixer(hf_l, hb_l, za_l, of_l, ob_l, zb_l, yc_l, yd_l, xl, m_l, tm, cap_l)
        if not last:
            yc_c = _ctx_attention(qc_c, sink, True)
            yd_c = _ctx_attention(qd_c, sink, False)
            xc = after_mixer(hf_c, hb_c, za_c, of_c, ob_c, zb_c, yc_c, yd_c, xc, m_c, tmc, cap_c)
    return _final_norm(xl, final_norm_w[None, :], tm)
```

```python
import functools
import math

import jax
import jax.numpy as jnp
import numpy as np
from jax import lax
from jax.experimental import pallas as pl
from jax.experimental.pallas import tpu as pltpu

D_MODEL = 1024
GRID_W = 64
GROUP_W = D_MODEL // 4
HEAD_DIM = 64
EPS = 1e-6
A_BLOCKS = 4
CONV_W = 4
LRU_C = 8.0
B_HEADS = 4
WINDOW = 128
ROPE_THETA = 10000.0
N_EXPERTS = 16
CAP_FACTOR = 2
D_EXPERT = 1024
D_IN = 2816
N_MOD = 6

LANES = 128
SUBLANES = 8
VMEM_LIMIT_BYTES = 56 * 1024 * 1024

NEG_BIG = -1e30
LOG2_E = math.log2(math.e)
F32 = jnp.float32
BF16 = jnp.bfloat16


def _cparams(sem):
    return pltpu.CompilerParams(dimension_semantics=sem, vmem_limit_bytes=VMEM_LIMIT_BYTES)


def _dot(a, b):
    return jnp.dot(a, b, preferred_element_type=F32)


def _dot_nt(a, b):
    return lax.dot_general(a, b, (((1,), (1,)), ((), ())), preferred_element_type=F32)


def _silu(x):
    return x * (1.0 / (1.0 + jnp.exp(-x)))


def _sigmoid(x):
    return 1.0 / (1.0 + jnp.exp(-x))


def _iota(shape, dim):
    return lax.broadcasted_iota(jnp.int32, shape, dim)


def _mod_kernel(s_ref, w_ref, b_ref, o_ref):
    s = _silu(s_ref[...])
    o_ref[0] = jnp.dot(s, w_ref[0], precision=lax.Precision.HIGHEST, preferred_element_type=F32) + b_ref[0]


def _modulation(s_rows, mod_w, mod_b):
    depth, d, n = mod_w.shape
    tn = 1536
    return pl.pallas_call(
        _mod_kernel,
        grid=(depth, n // tn),
        in_specs=[pl.BlockSpec((SUBLANES, d), lambda l, j: (0, 0)),
                  pl.BlockSpec((1, d, tn), lambda l, j: (l, 0, j)),
                  pl.BlockSpec((1, 1, tn), lambda l, j: (l, 0, j))],
        out_specs=pl.BlockSpec((1, SUBLANES, tn), lambda l, j: (l, 0, j)),
        out_shape=jax.ShapeDtypeStruct((depth, SUBLANES, n), F32),
        compiler_params=_cparams(("parallel", "parallel")),
        name="modulation",
    )(s_rows, mod_w, mod_b.reshape(depth, 1, n))


def _rope(x, cos, sin_signed):
    w = x.shape[-1]
    lane = _iota(x.shape, 1)
    partner = jnp.where((lane % 32) < 16, pltpu.roll(x, w - 16, 1), pltpu.roll(x, 16, 1))
    return x * cos + partner * sin_signed


def _head_rms(x, bd_mean, w):
    ms = _dot((x * x).astype(BF16), bd_mean)
    return x * lax.rsqrt(ms + EPS) * w


def _pack_qkv(o_ref, q, k, v):
    o_ref[0, :, 0:256] = q.astype(BF16)
    o_ref[0, :, 256:384] = k.astype(BF16)
    o_ref[0, :, 384:512] = pltpu.roll(k, 64, 1).astype(BF16)
    o_ref[0, :, 512:640] = v.astype(BF16)
    o_ref[0, :, 640:768] = pltpu.roll(v, 64, 1).astype(BF16)


def _inproj_kernel(x_ref, nw_ref, sh_ref, sc_ref, w_ref, cos_ref, sin_ref, qw_ref, kw_ref, bd_ref,
                   za_ref, zb_ref, qc_ref, qd_ref):
    x = x_ref[0]
    y = x * lax.rsqrt(jnp.mean(x * x, axis=-1, keepdims=True) + EPS) * nw_ref[...]
    h = y * (1.0 + sc_ref[0]) + sh_ref[0]
    z = _dot(h.astype(BF16), w_ref[...])
    za_ref[0] = z[:, 0:512]
    zb_ref[0] = z[:, 512:1792]
    cos = cos_ref[...]
    sin = sin_ref[...]
    scale = HEAD_DIM ** -0.5
    cq = _rope(z[:, 1792:2048], cos, sin) * scale
    ck = _rope(z[:, 2048:2176], cos[:, 0:128], sin[:, 0:128])
    _pack_qkv(qc_ref, cq, ck, z[:, 2176:2304])
    bd = bd_ref[...]
    dq = _rope(_head_rms(z[:, 2304:2560], bd, qw_ref[...]), cos, sin) * (scale * LOG2_E)
    dk = _rope(_head_rms(z[:, 2560:2688], bd[0:128, 0:128], kw_ref[...]), cos[:, 0:128], sin[:, 0:128])
    _pack_qkv(qd_ref, dq, dk, z[:, 2688:2816])


def _inproj(x, norm_w, shift, scale, w_in_bf, cos, sin, qw, kw, bd_mean, tm):
    b, n, _ = x.shape
    d = D_MODEL
    row = lambda bi, i: (bi, i, 0)
    const2 = lambda bi, i: (0, 0)
    return pl.pallas_call(
        _inproj_kernel,
        grid=(b, n // tm),
        in_specs=[pl.BlockSpec((1, tm, d), row),
                  pl.BlockSpec((1, d), const2),
                  pl.BlockSpec((1, 1, d), lambda bi, i: (bi, 0, 0)),
                  pl.BlockSpec((1, 1, d), lambda bi, i: (bi, 0, 0)),
                  pl.BlockSpec((d, D_IN), const2),
                  pl.BlockSpec((tm, 256), lambda bi, i: (i, 0)),
                  pl.BlockSpec((tm, 256), lambda bi, i: (i, 0)),
                  pl.BlockSpec((1, 256), const2),
                  pl.BlockSpec((1, 128), const2),
                  pl.BlockSpec((256, 256), const2)],
        out_specs=[pl.BlockSpec((1, tm, 512), row), pl.BlockSpec((1, tm, 1280), row),
                   pl.BlockSpec((1, tm, 768), row), pl.BlockSpec((1, tm, 768), row)],
        out_shape=[jax.ShapeDtypeStruct((b, n, 512), F32), jax.ShapeDtypeStruct((b, n, 1280), F32),
                   jax.ShapeDtypeStruct((b, n, 768), BF16), jax.ShapeDtypeStruct((b, n, 768), BF16)],
        compiler_params=_cparams(("parallel", "parallel")),
        name="inproj",
    )(x, norm_w, shift, scale, w_in_bf, cos, sin, qw, kw, bd_mean)


def _lru_kernel(fc_ref, fp_ref, fn_ref, bc_ref, bp_ref, bn_ref, h0_ref, cw_ref, cb_ref, gw_ref, gb_ref, lam_ref,
                hf_ref, hb_ref, hl_ref, carry_ref, *, t, nt):
    i = pl.program_id(1)

    @pl.when(i == 0)
    def _():
        carry_ref[...] = h0_ref[0]

    row = _iota((t, GROUP_W), 0)
    cw = cw_ref[...]

    def coeffs(cur_ref, prev_ref, next_ref, tile, d):
        prev = jnp.where(tile == 0, 0.0, prev_ref[0])
        nxt = jnp.where(tile == nt - 1, 0.0, next_ref[0])
        ext = jnp.concatenate([prev, cur_ref[0], nxt], axis=0)
        n_ext = t + 2 * SUBLANES
        xa = cb_ref[...] + cw[2:3] * ext[SUBLANES:SUBLANES + t]
        for j, off in ((0, 2), (1, 1), (3, -1)):
            xa = xa + cw[j:j + 1] * pltpu.roll(ext, off % n_ext, 0)[SUBLANES:SUBLANES + t]
        g = _dot(xa.astype(BF16), gw_ref[d]) + gb_ref[d]
        r = _sigmoid(g[:, 0:GROUP_W])
        gi = _sigmoid(g[:, GROUP_W:2 * GROUP_W])
        lam = lam_ref[d]
        softplus = jnp.maximum(-lam, 0.0) + jnp.log(1.0 + jnp.exp(-jnp.abs(lam)))
        log_a = (-LRU_C) * r * softplus
        a = jnp.exp(log_a)
        u = jnp.sqrt(1.0 - jnp.exp(2.0 * log_a)) * (gi * xa)
        return a, u

    a, u = coeffs(fc_ref, fp_ref, fn_ref, i, 0)
    s = 1
    while s < t:
        keep = row >= s
        a_sh = jnp.where(keep, pltpu.roll(a, s, 0), 1.0)
        u_sh = jnp.where(keep, pltpu.roll(u, s, 0), 0.0)
        u = a * u_sh + u
        a = a * a_sh
        s *= 2
    h = u + a * carry_ref[0:1]
    hf_ref[0] = h
    carry_ref[0:1] = hf_ref[0, t - 1:t, :]

    a, u = coeffs(bc_ref, bp_ref, bn_ref, nt - 1 - i, 1)
    s = 1
    while s < t:
        keep = row < t - s
        a_sh = jnp.where(keep, pltpu.roll(a, t - s, 0), 1.0)
        u_sh = jnp.where(keep, pltpu.roll(u, t - s, 0), 0.0)
        u = a * u_sh + u
        a = a * a_sh
        s *= 2
    h = u + a * carry_ref[1:2]
    hb_ref[0] = h
    carry_ref[1:2] = hb_ref[0, 0:1, :]

    @pl.when(i == nt - 1)
    def _():
        hl_ref[0] = carry_ref[...]


def _lru(za, h0, conv_w, conv_b, gate_w, gate_b, lam, t):
    b, n, _ = za.shape
    nt = n // t
    t8 = t // SUBLANES
    n8 = n // SUBLANES
    cur_f = lambda bi, i: (bi, i, 0)
    prev_f = lambda bi, i: (bi, jnp.maximum(i * t8 - 1, 0), 0)
    next_f = lambda bi, i: (bi, jnp.minimum((i + 1) * t8, n8 - 1), 0)
    cur_b = lambda bi, i: (bi, nt - 1 - i, 0)
    prev_b = lambda bi, i: (bi, jnp.maximum((nt - 1 - i) * t8 - 1, 0), 0)
    next_b = lambda bi, i: (bi, jnp.minimum((nt - i) * t8, n8 - 1), 0)
    c2 = lambda bi, i: (0, 0)
    c3 = lambda bi, i: (0, 0, 0)
    halo = (1, SUBLANES, GROUP_W)
    return pl.pallas_call(
        functools.partial(_lru_kernel, t=t, nt=nt),
        grid=(b, nt),
        in_specs=[pl.BlockSpec((1, t, GROUP_W), cur_f), pl.BlockSpec(halo, prev_f), pl.BlockSpec(halo, next_f),
                  pl.BlockSpec((1, t, GROUP_W), cur_b), pl.BlockSpec(halo, prev_b), pl.BlockSpec(halo, next_b),
                  pl.BlockSpec((1, SUBLANES, GROUP_W), lambda bi, i: (bi, 0, 0)),
                  pl.BlockSpec((SUBLANES, GROUP_W), c2), pl.BlockSpec((1, GROUP_W), c2),
                  pl.BlockSpec((2, GROUP_W, 2 * GROUP_W), c3), pl.BlockSpec((2, 1, 2 * GROUP_W), c3),
                  pl.BlockSpec((2, 1, GROUP_W), c3)],
        out_specs=[pl.BlockSpec((1, t, GROUP_W), cur_f), pl.BlockSpec((1, t, GROUP_W), cur_b),
                   pl.BlockSpec((1, SUBLANES, GROUP_W), lambda bi, i: (bi, 0, 0))],
        out_shape=[jax.ShapeDtypeStruct((b, n, GROUP_W), F32), jax.ShapeDtypeStruct((b, n, GROUP_W), F32),
                   jax.ShapeDtypeStruct((b, SUBLANES, GROUP_W), F32)],
        scratch_shapes=[pltpu.VMEM((SUBLANES, GROUP_W), F32)],
        compiler_params=_cparams(("parallel", "arbitrary")),
        name="rglru",
    )(za, za, za, za, za, za, h0, conv_w, conv_b, gate_w, gate_b, lam)


B_CHUNK = 128
B_SUB = 32


def _hgrn_chunk(q_raw, f_raw, v, lb, st_ref, d, b_ref, k_ref, v_ref, bd_ones, reverse):
    t = B_CHUNK
    n_sub = t // B_SUB
    q = _silu(q_raw) * (GROUP_W // B_HEADS) ** -0.5
    log_sig = jnp.minimum(f_raw, 0.0) - jnp.log(1.0 + jnp.exp(-jnp.abs(f_raw)))
    log_lb = jnp.log(lb)
    y = jnp.log(1.0 - lb) + log_sig
    log_f = jnp.maximum(log_lb, y) + jnp.log(1.0 + jnp.exp(-jnp.abs(log_lb - y)))
    k = (1.0 - lb) * _sigmoid(-f_raw)
    row = _iota((t, GROUP_W), 0)
    b = log_f * LOG2_E
    s = 1
    while s < t:
        if reverse:
            b = b + jnp.where(row < t - s, pltpu.roll(b, t - s, 0), 0.0)
        else:
            b = b + jnp.where(row >= s, pltpu.roll(b, s, 0), 0.0)
        s *= 2
    b_ref[...] = b
    k_ref[...] = k
    v_ref[...] = v
    edge = 0 if reverse else t - 1
    b_edge = b_ref[edge:edge + 1, :]
    st = st_ref[d]
    lane = _iota((B_SUB, GROUP_W), 1)
    head_masks = [(lane >= 64 * hh) & (lane < 64 * (hh + 1)) for hh in range(B_HEADS)]
    trow = _iota((B_SUB, GROUP_W), 0)
    srow = _iota((t, GROUP_W), 0)

    o_inter = _dot_nt((q * jnp.exp2(b)).astype(BF16), st.astype(BF16))
    outs = []
    for i in range(n_sub):
        lo = i * B_SUB
        qb = q[lo:lo + B_SUB]
        bb = b[lo:lo + B_SUB]
        acc = o_inter[lo:lo + B_SUB]
        has_off = (i < n_sub - 1) if reverse else (i > 0)
        if has_off:
            ref_row = lo + B_SUB if reverse else lo - 1
            r_i = b_ref[ref_row:ref_row + 1, :]
            qh = qb * jnp.exp2(bb - r_i)
            key_ok = (srow >= lo + B_SUB) if reverse else (srow < lo)
            kh = jnp.where(key_ok, k * jnp.exp2(jnp.minimum(r_i - b, 0.0)), 0.0)
            qstack = jnp.concatenate([jnp.where(m, qh, 0.0) for m in head_masks], axis=0).astype(BF16)
            att = _dot_nt(qstack, kh.astype(BF16))
            res = _dot(att.astype(BF16), v.astype(BF16))
            for hh in range(B_HEADS):
                acc = acc + jnp.where(head_masks[hh], res[hh * B_SUB:(hh + 1) * B_SUB], 0.0)
        pieces = []
        for sl in range(B_SUB):
            srow_b = b_ref[lo + sl:lo + sl + 1, :]
            srow_k = k_ref[lo + sl:lo + sl + 1, :]
            ok = (trow <= sl) if reverse else (trow >= sl)
            e = jnp.exp2(jnp.where(ok, bb - srow_b, NEG_BIG)) * (qb * srow_k)
            pieces.append(e.astype(BF16))
        g = _dot(jnp.concatenate(pieces, axis=0), bd_ones)
        for sl in range(B_SUB):
            acc = acc + g[sl * B_SUB:(sl + 1) * B_SUB] * v_ref[lo + sl:lo + sl + 1, :]
        outs.append(acc)
    kt = k * jnp.exp2(b_edge - b)
    upd = _dot(jnp.transpose(v).astype(BF16), kt.astype(BF16))
    vrow = _iota((GROUP_W, GROUP_W), 0) // 64
    kcol = _iota((GROUP_W, GROUP_W), 1) // 64
    st_ref[d] = st * jnp.exp2(b_edge) + jnp.where(vrow == kcol, upd, 0.0)
    return jnp.concatenate(outs, axis=0)


def _hgrn_kernel(qf_ref, ff_ref, vf_ref, qb_ref, fb_ref, vb_ref, s0_ref, lb_ref, bd_ref,
                 of_ref, ob_ref, sl_ref, st_ref, b_ref, k_ref, v_ref, *, nt):
    i = pl.program_id(1)

    @pl.when(i == 0)
    def _():
        st_ref[...] = s0_ref[0]

    lb = lb_ref[...]
    bd = bd_ref[...]
    of_ref[0] = _hgrn_chunk(qf_ref[0], ff_ref[0], vf_ref[0], lb, st_ref, 0, b_ref, k_ref, v_ref, bd, False)
    ob_ref[0] = _hgrn_chunk(qb_ref[0], fb_ref[0], vb_ref[0], lb, st_ref, 1, b_ref, k_ref, v_ref, bd, True)

    @pl.when(i == nt - 1)
    def _():
        sl_ref[0] = st_ref[...]


def _hgrn(zb, s0, lb, bd_ones):
    b, n, _ = zb.shape
    t = B_CHUNK
    nt = n // t
    blk = (1, t, GROUP_W)
    fwd = lambda col: (lambda bi, i: (bi, i, col))
    bwd = lambda col: (lambda bi, i: (bi, nt - 1 - i, col))
    state_spec = pl.BlockSpec((1, 2, GROUP_W, GROUP_W), lambda bi, i: (bi, 0, 0, 0))
    return pl.pallas_call(
        functools.partial(_hgrn_kernel, nt=nt),
        grid=(b, nt),
        in_specs=[pl.BlockSpec(blk, fwd(0)), pl.BlockSpec(blk, fwd(1)), pl.BlockSpec(blk, fwd(3)),
                  pl.BlockSpec(blk, bwd(0)), pl.BlockSpec(blk, bwd(2)), pl.BlockSpec(blk, bwd(3)),
                  state_spec,
                  pl.BlockSpec((1, GROUP_W), lambda bi, i: (0, 0)),
                  pl.BlockSpec((GROUP_W, GROUP_W), lambda bi, i: (0, 0))],
        out_specs=[pl.BlockSpec(blk, fwd(0)), pl.BlockSpec(blk, bwd(0)), state_spec],
        out_shape=[jax.ShapeDtypeStruct((b, n, GROUP_W), F32), jax.ShapeDtypeStruct((b, n, GROUP_W), F32),
                   jax.ShapeDtypeStruct((b, 2, GROUP_W, GROUP_W), F32)],
        scratch_shapes=[pltpu.VMEM((2, GROUP_W, GROUP_W), F32), pltpu.VMEM((t, GROUP_W), F32),
                        pltpu.VMEM((t, GROUP_W), F32), pltpu.VMEM((t, GROUP_W), F32)],
        compiler_params=_cparams(("parallel", "arbitrary")),
        name="hgrn2",
    )(zb, zb, zb, zb, zb, zb, s0, lb, bd_ones)


def _stack_q(q):
    tq = q.shape[0]
    lane = _iota((tq, LANES), 1)
    lo = lane < 64
    q0 = q[:, 0:128]
    q1 = q[:, 128:256]
    zero = jnp.zeros_like(q0)
    s1 = jnp.concatenate([jnp.where(lo, q0, zero), jnp.where(lo, zero, q1)], axis=0)
    s2 = jnp.concatenate([jnp.where(lo, zero, q0), jnp.where(lo, q1, zero)], axis=0)
    return s1, s2


def _unstack_out(r1, r2, tq):
    lane = _iota((tq, LANES), 1)
    lo = lane < 64
    o0 = jnp.where(lo, r1[0:tq], r2[0:tq])
    o1 = jnp.where(lo, r2[tq:2 * tq], r1[tq:2 * tq])
    return jnp.concatenate([o0, o1], axis=1)


def _sink_cols(sink_ref, tq):
    row = _iota((2 * tq, 1), 0)
    c1 = jnp.where(row < tq, sink_ref[0], sink_ref[3])
    c2 = jnp.where(row < tq, sink_ref[1], sink_ref[2])
    return c1, c2


def _window_kernel(sink_ref, q_ref, kp_ref, kc_ref, kn_ref, vp_ref, vc_ref, vn_ref, kx_ref, vx_ref, o_ref,
                   *, tq, n):
    i = pl.program_id(1)
    s1, s2 = _stack_q(q_ref[0])
    kband = jnp.concatenate([kp_ref[0], kc_ref[0], kn_ref[0]], axis=0)
    vband = jnp.concatenate([vp_ref[0], vc_ref[0], vn_ref[0]], axis=0)
    nband = tq + 2 * WINDOW
    qpos = i * tq + (_iota((2 * tq, nband), 0) % tq)
    kpos = i * tq - WINDOW + _iota((2 * tq, nband), 1)
    ok = (kpos >= 0) & (kpos < n) & (jnp.abs(qpos - kpos) <= WINDOW)
    sinks = _sink_cols(sink_ref, tq)
    res = []
    for qs, half, sink in ((s1, 0, sinks[0]), (s2, 1, sinks[1])):
        kx = kx_ref[0][:, half * 128:(half + 1) * 128]
        vx = vx_ref[0][:, half * 128:(half + 1) * 128]
        sc = _dot_nt(qs, kx)
        sw = jnp.where(ok, _dot_nt(qs, kband[:, half * 128:(half + 1) * 128]), NEG_BIG)
        m = jnp.maximum(jnp.maximum(jnp.max(sc, axis=-1, keepdims=True), jnp.max(sw, axis=-1, keepdims=True)), sink)
        pc = jnp.exp(sc - m)
        pw = jnp.exp(sw - m)
        den = jnp.sum(pc, axis=-1, keepdims=True) + jnp.sum(pw, axis=-1, keepdims=True) + jnp.exp(sink - m)
        num = _dot(pc.astype(BF16), vx) + _dot(pw.astype(BF16), vband[:, half * 128:(half + 1) * 128])
        res.append(num / den)
    o_ref[0] = _unstack_out(res[0], res[1], tq).astype(BF16)


def _window_attention(qkv, qkv_ctx, sink, tq):
    b, n, _ = qkv.shape
    lc = qkv_ctx.shape[1]
    w = WINDOW
    r = tq // w
    nw = n // w
    prev = lambda col: (lambda bi, i, s: (bi, jnp.maximum(i * r - 1, 0), col))
    cur = lambda col: (lambda bi, i, s: (bi, i, col))
    nxt = lambda col: (lambda bi, i, s: (bi, jnp.minimum((i + 1) * r, nw - 1), col))
    ctx = lambda col: (lambda bi, i, s: (bi, 0, col))
    grid_spec = pltpu.PrefetchScalarGridSpec(
        num_scalar_prefetch=1,
        grid=(b, n // tq),
        in_specs=[pl.BlockSpec((1, tq, 256), cur(0)),
                  pl.BlockSpec((1, w, 256), prev(1)), pl.BlockSpec((1, tq, 256), cur(1)),
                  pl.BlockSpec((1, w, 256), nxt(1)),
                  pl.BlockSpec((1, w, 256), prev(2)), pl.BlockSpec((1, tq, 256), cur(2)),
                  pl.BlockSpec((1, w, 256), nxt(2)),
                  pl.BlockSpec((1, lc, 256), ctx(1)), pl.BlockSpec((1, lc, 256), ctx(2))],
        out_specs=pl.BlockSpec((1, tq, 256), cur(0)),
    )
    return pl.pallas_call(
        functools.partial(_window_kernel, tq=tq, n=n),
        grid_spec=grid_spec,
        out_shape=jax.ShapeDtypeStruct((b, n, 256), BF16),
        compiler_params=_cparams(("parallel", "parallel")),
        name="window_attn",
    )(sink, qkv, qkv, qkv, qkv, qkv, qkv, qkv, qkv_ctx, qkv_ctx)


def _ctx_attn_kernel(sink_ref, q_ref, k_ref, v_ref, o_ref, *, tq, use_sink):
    s1, s2 = _stack_q(q_ref[0])
    sinks = _sink_cols(sink_ref, tq)
    res = []
    for qs, half, sink in ((s1, 0, sinks[0]), (s2, 1, sinks[1])):
        kx = k_ref[0][:, half * 128:(half + 1) * 128]
        vx = v_ref[0][:, half * 128:(half + 1) * 128]
        sc = _dot_nt(qs, kx)
        m = jnp.max(sc, axis=-1, keepdims=True)
        if use_sink:
            m = jnp.maximum(m, sink)
        p = jnp.exp(sc - m) if use_sink else jnp.exp2(sc - m)
        den = jnp.sum(p, axis=-1, keepdims=True)
        if use_sink:
            den = den + jnp.exp(sink - m)
        res.append(_dot(p.astype(BF16), vx) / den)
    o_ref[0] = _unstack_out(res[0], res[1], tq).astype(BF16)


def _ctx_attention(qkv_ctx, sink, use_sink):
    b, lc, _ = qkv_ctx.shape
    col = lambda c: (lambda bi, s: (bi, 0, c))
    grid_spec = pltpu.PrefetchScalarGridSpec(
        num_scalar_prefetch=1,
        grid=(b,),
        in_specs=[pl.BlockSpec((1, lc, 256), col(0)), pl.BlockSpec((1, lc, 256), col(1)),
                  pl.BlockSpec((1, lc, 256), col(2))],
        out_specs=pl.BlockSpec((1, lc, 256), col(0)),
    )
    return pl.pallas_call(
        functools.partial(_ctx_attn_kernel, tq=lc, use_sink=use_sink),
        grid_spec=grid_spec,
        out_shape=jax.ShapeDtypeStruct((b, lc, 256), BF16),
        compiler_params=_cparams(("parallel",)),
        name="ctx_attn",
    )(sink, qkv_ctx, qkv_ctx, qkv_ctx)


def _dense_kernel(q_ref, k_ref, v_ref, kx_ref, vx_ref, o_ref, qs_ref, m_ref, acc_ref, *, tq, tk, ck, rb, nk):
    ki = pl.program_id(2)
    rows = 2 * tq

    def update(half, kc, vc):
        n = kc.shape[0]
        vext = jnp.concatenate([vc, jnp.ones((n, LANES), BF16)], axis=1)
        for r in range(rows // rb):
            sl = slice(r * rb, (r + 1) * rb)
            s = _dot_nt(qs_ref[half, sl, :], kc)
            m_old = m_ref[half, sl, :]
            m_new = jnp.maximum(m_old, jnp.max(s, axis=-1, keepdims=True))
            alpha = jnp.exp2(m_old - m_new)
            p = jnp.exp2(s - jnp.tile(m_new, (1, n // LANES))).astype(BF16)
            m_ref[half, sl, :] = m_new
            acc_ref[half, sl, :] = acc_ref[half, sl, :] * jnp.tile(alpha, (1, 2)) + _dot(p, vext)

    @pl.when(ki == 0)
    def _():
        s1, s2 = _stack_q(q_ref[0])
        qs_ref[0] = s1
        qs_ref[1] = s2
        m_ref[...] = jnp.full(m_ref.shape, NEG_BIG, F32)
        acc_ref[...] = jnp.zeros(acc_ref.shape, F32)
        for half in range(2):
            update(half, kx_ref[0][:, half * 128:(half + 1) * 128], vx_ref[0][:, half * 128:(half + 1) * 128])

    for c in range(tk // ck):
        for half in range(2):
            update(half, k_ref[0, c * ck:(c + 1) * ck, half * 128:(half + 1) * 128],
                   v_ref[0, c * ck:(c + 1) * ck, half * 128:(half + 1) * 128])

    @pl.when(ki == nk - 1)
    def _():
        a1 = acc_ref[0]
        a2 = acc_ref[1]
        r1 = a1[:, 0:LANES] / a1[:, LANES:2 * LANES]
        r2 = a2[:, 0:LANES] / a2[:, LANES:2 * LANES]
        o_ref[0] = _unstack_out(r1, r2, tq).astype(BF16)


def _dense_attention(qkv, qkv_ctx, tq, tk, ck):
    b, n, _ = qkv.shape
    lc = qkv_ctx.shape[1]
    nk = n // tk
    rb = min(2 * tq, 256)
    return pl.pallas_call(
        functools.partial(_dense_kernel, tq=tq, tk=tk, ck=ck, rb=rb, nk=nk),
        grid=(b, n // tq, nk),
        in_specs=[pl.BlockSpec((1, tq, 256), lambda bi, qi, ki: (bi, qi, 0)),
                  pl.BlockSpec((1, tk, 256), lambda bi, qi, ki: (bi, ki, 1)),
                  pl.BlockSpec((1, tk, 256), lambda bi, qi, ki: (bi, ki, 2)),
                  pl.BlockSpec((1, lc, 256), lambda bi, qi, ki: (bi, 0, 1)),
                  pl.BlockSpec((1, lc, 256), lambda bi, qi, ki: (bi, 0, 2))],
        out_specs=pl.BlockSpec((1, tq, 256), lambda bi, qi, ki: (bi, qi, 0)),
        out_shape=jax.ShapeDtypeStruct((b, n, 256), BF16),
        scratch_shapes=[pltpu.VMEM((2, 2 * tq, LANES), BF16), pltpu.VMEM((2, 2 * tq, LANES), F32),
                        pltpu.VMEM((2, 2 * tq, 2 * LANES), F32)],
        compiler_params=_cparams(("parallel", "parallel", "arbitrary")),
        name="dense_attn",
    )(qkv, qkv, qkv, qkv_ctx, qkv_ctx)


def _gelu_tanh(x):
    return 0.5 * x * (1.0 + jnp.tanh(math.sqrt(2.0 / math.pi) * (x + 0.044715 * (x * x * x))))


def _outproj_kernel(hf_ref, hb_ref, ay_ref, of_ref, ob_ref, bg_ref, yc_ref, yd_ref, x_ref, w_ref, m_ref, nw_ref,
                    ow_ref, bd_ref, rw_ref, xh_ref, aff_ref):
    ya = (hf_ref[0] + hb_ref[0]) * _gelu_tanh(ay_ref[0])
    o = of_ref[0] + ob_ref[0]
    ms = _dot((o * o).astype(BF16), bd_ref[...])
    yb = o * lax.rsqrt(ms + EPS) * ow_ref[...] * _silu(bg_ref[0])
    y = jnp.concatenate([ya.astype(BF16), yb.astype(BF16), yc_ref[0], yd_ref[0]], axis=1)
    out = _dot(y, w_ref[...])
    m = m_ref[0]
    x = x_ref[0] + m[0:1] * out
    hn = x * lax.rsqrt(jnp.mean(x * x, axis=-1, keepdims=True) + EPS) * nw_ref[...]
    h2 = hn * (1.0 + m[2:3]) + m[1:2]
    xh_ref[0, :, 0:D_MODEL] = x
    xh_ref[0, :, D_MODEL:2 * D_MODEL] = h2
    logits = lax.dot_general(rw_ref[...], h2, (((1,), (1,)), ((), ())), precision=lax.Precision.HIGHEST,
                             preferred_element_type=F32)
    mx = jnp.max(logits, axis=0, keepdims=True)
    e = jnp.exp(logits - mx)
    aff_ref[0] = e / jnp.sum(e, axis=0, keepdims=True)


def _outproj(hf, hb, za, of, ob, zb, yc, yd, x, w_out_bf, mrows, norm2_w, onorm_w, bd_mean, router_t, tm):
    b, n, _ = x.shape
    d = D_MODEL
    row = lambda col: (lambda bi, i: (bi, i, col))
    c2 = lambda bi, i: (0, 0)
    g = (1, tm, GROUP_W)
    return pl.pallas_call(
        _outproj_kernel,
        grid=(b, n // tm),
        in_specs=[pl.BlockSpec(g, row(0)), pl.BlockSpec(g, row(0)), pl.BlockSpec(g, row(1)),
                  pl.BlockSpec(g, row(0)), pl.BlockSpec(g, row(0)), pl.BlockSpec(g, row(4)),
                  pl.BlockSpec(g, row(0)), pl.BlockSpec(g, row(0)),
                  pl.BlockSpec((1, tm, d), row(0)),
                  pl.BlockSpec((d, d), c2),
                  pl.BlockSpec((1, SUBLANES, d), lambda bi, i: (bi, 0, 0)),
                  pl.BlockSpec((1, d), c2), pl.BlockSpec((1, GROUP_W), c2),
                  pl.BlockSpec((GROUP_W, GROUP_W), c2), pl.BlockSpec((N_EXPERTS, d), c2)],
        out_specs=[pl.BlockSpec((1, tm, 2 * d), row(0)),
                   pl.BlockSpec((1, N_EXPERTS, tm), lambda bi, i: (bi, 0, i))],
        out_shape=[jax.ShapeDtypeStruct((b, n, 2 * d), F32),
                   jax.ShapeDtypeStruct((b, N_EXPERTS, n), F32)],
        compiler_params=_cparams(("parallel", "parallel")),
        name="outproj",
    )(hf, hb, za, of, ob, zb, yc, yd, x, w_out_bf, mrows, norm2_w, onorm_w, bd_mean, router_t)


def _topk_kernel(aff_ref, idx_ref, gate_ref, *, gp, cap):
    x_all = aff_ref[0]
    xb_all = pltpu.bitcast(x_all, jnp.int32)
    kf = float(cap)

    lo = jnp.zeros((N_EXPERTS, 1, 1), jnp.int32)
    hi = jnp.full((N_EXPERTS, 1, 1), 0x7F800000, jnp.int32)

    def bisect(_, carry):
        lo, hi = carry
        mid = lo + lax.shift_right_logical(hi - lo, 1)
        cnt = jnp.sum(jnp.sum(jnp.where(xb_all >= mid, 1.0, 0.0), axis=2, keepdims=True), axis=1, keepdims=True)
        ge = cnt >= kf
        return jnp.where(ge, mid, lo), jnp.where(ge, hi, mid)

    lo, hi = lax.fori_loop(0, 31, bisect, (lo, hi))

    upper = jnp.where(_iota((LANES, LANES), 0) <= _iota((LANES, LANES), 1), 1.0, 0.0).astype(BF16)
    strict = jnp.where(_iota((gp, gp), 1) < _iota((gp, gp), 0), 1.0, 0.0).astype(BF16)
    g_col = _iota((gp, 1), 0).astype(F32)
    j_col = _iota((LANES, 1), 0).astype(F32)
    slot = _iota((1, cap), 1).astype(F32)

    def prefix(mask_f):
        within = _dot(mask_f.astype(BF16), upper)
        before = jnp.sum(_dot(strict, mask_f.astype(BF16)), axis=1, keepdims=True)
        return within, before

    for e in range(N_EXPERTS):
        x = x_all[e]
        xb = xb_all[e]
        thr = lo[e]
        gt = jnp.where(xb > thr, 1.0, 0.0)
        eq = jnp.where(xb == thr, 1.0, 0.0)
        need = kf - jnp.sum(jnp.sum(gt, axis=1, keepdims=True), axis=0, keepdims=True)
        eq_within, eq_before = prefix(eq)
        tie_rank = eq_before + eq_within - eq
        sel = jnp.maximum(gt, jnp.where(tie_rank < need, eq, 0.0))
        within, before = prefix(sel)
        incl = before + within[:, LANES - 1:LANES]
        g_of_s = jnp.sum(jnp.where(incl <= slot, 1.0, 0.0), axis=0, keepdims=True)
        base = jnp.max(jnp.where(before <= slot, before, 0.0), axis=0, keepdims=True)
        rank = slot - base
        onehot = jnp.where(g_col == g_of_s, 1.0, 0.0).astype(BF16)
        counts = _dot(jnp.transpose(within).astype(BF16), onehot)
        j_of_s = jnp.sum(jnp.where(counts <= rank, 1.0, 0.0), axis=0, keepdims=True)
        idx_ref[0, e] = (g_of_s * float(LANES) + j_of_s).astype(jnp.int32)
        xt = jnp.transpose(x)
        x_hi = xt.astype(BF16)
        r1 = xt - x_hi.astype(F32)
        x_mid = r1.astype(BF16)
        x_lo = (r1 - x_mid.astype(F32)).astype(BF16)
        vals = _dot(x_hi, onehot) + _dot(x_mid, onehot) + _dot(x_lo, onehot)
        gate_ref[0, e] = jnp.sum(jnp.where(j_col == j_of_s, vals, 0.0), axis=0, keepdims=True)


def _route(aff_t, n_tokens, cap):
    b = aff_t.shape[0]
    g = n_tokens // LANES
    gp = -(-g // LANES) * LANES
    aff4 = aff_t.reshape(b, N_EXPERTS, g, LANES)
    if gp != g:
        aff4 = jnp.pad(aff4, ((0, 0), (0, 0), (0, gp - g), (0, 0)), constant_values=-1.0)
    return pl.pallas_call(
        functools.partial(_topk_kernel, gp=gp, cap=cap),
        grid=(b,),
        in_specs=[pl.BlockSpec((1, N_EXPERTS, gp, LANES), lambda bi: (bi, 0, 0, 0))],
        out_specs=[pl.BlockSpec((1, N_EXPERTS, 1, cap), lambda bi: (bi, 0, 0, 0)),
                   pl.BlockSpec((1, N_EXPERTS, 1, cap), lambda bi: (bi, 0, 0, 0))],
        out_shape=[jax.ShapeDtypeStruct((b, N_EXPERTS, 1, cap), jnp.int32),
                   jax.ShapeDtypeStruct((b, N_EXPERTS, 1, cap), F32)],
        compiler_params=_cparams(("parallel",)),
        name="route_topk",
    )(aff4)


def _ffn_kernel(idx_ref, xh_in, gate_ref, m_ref, w1_ref, w3_ref, w2_ref, xh_out,
                buf, wbf, sem_g, sem_o, *, cap, rows):
    b = pl.program_id(1)
    del xh_in
    d = D_MODEL
    n_chunks = cap // rows

    @pl.when(b == 0)
    def _():
        wbf[0] = w1_ref[0, 0].astype(BF16)
        wbf[1] = w3_ref[0, 0].astype(BF16)
        wbf[2] = w2_ref[0, 0].astype(BF16)

    def gather(c, slot):
        def body(r, _):
            tok = idx_ref[0, 0, c * rows + r]
            pltpu.make_async_copy(xh_out.at[b, pl.ds(tok, 1), :], buf.at[slot, pl.ds(r, 1), :],
                                  sem_g.at[slot]).start()
            return 0

        lax.fori_loop(0, rows, body, 0, unroll=8)

    def wait_gather(slot):
        pltpu.make_async_copy(xh_out.at[b, pl.ds(0, rows), :], buf.at[slot], sem_g.at[slot]).wait()

    def scatter(c, slot):
        def body(r, _):
            tok = idx_ref[0, 0, c * rows + r]
            pltpu.make_async_copy(buf.at[slot, pl.ds(r, 1), pl.ds(0, d)],
                                  xh_out.at[b, pl.ds(tok, 1), pl.ds(0, d)], sem_o.at[slot]).start()
            return 0

        lax.fori_loop(0, rows, body, 0, unroll=8)

    def wait_scatter(slot):
        pltpu.make_async_copy(buf.at[slot, :, pl.ds(0, d)], xh_out.at[b, pl.ds(0, rows), pl.ds(0, d)],
                              sem_o.at[slot]).wait()

    gather(0, 0)
    for c in range(n_chunks):
        slot = c % 2
        if c + 1 < n_chunks:
            if c >= 1:
                wait_scatter(1 - slot)
            gather(c + 1, 1 - slot)
        wait_gather(slot)
        xb = buf[slot, :, d:2 * d].astype(BF16)
        hid = _silu(_dot(xb, wbf[0])) * _dot(xb, wbf[1])
        y = _dot(hid.astype(BF16), wbf[2])
        y = y * gate_ref[0, 0, c * rows:(c + 1) * rows, :] * m_ref[0]
        buf[slot, :, 0:d] = buf[slot, :, 0:d] + y
        scatter(c, slot)
    if n_chunks >= 2:
        wait_scatter(n_chunks % 2)
    wait_scatter((n_chunks - 1) % 2)


def _expert_ffn(idx, gate, xh, mgate, w1, w3, w2, layer):
    b, n, d2 = xh.shape
    d = D_MODEL
    cap = idx.shape[-1]
    rows = min(cap, 256)
    wspec = lambda rows_, cols_: pl.BlockSpec((1, 1, rows_, cols_), lambda e, bi: (layer, e, 0, 0))
    return pl.pallas_call(
        functools.partial(_ffn_kernel, cap=cap, rows=rows),
        grid=(N_EXPERTS, b),
        in_specs=[pl.BlockSpec((1, 1, cap), lambda e, bi: (bi * N_EXPERTS + e, 0, 0), memory_space=pltpu.SMEM),
                  pl.BlockSpec(memory_space=pl.ANY),
                  pl.BlockSpec((1, 1, cap, 1), lambda e, bi: (bi, e, 0, 0)),
                  pl.BlockSpec((1, 1, d), lambda e, bi: (bi, 0, 0)),
                  wspec(d, D_EXPERT), wspec(d, D_EXPERT), wspec(D_EXPERT, d)],
        out_specs=pl.BlockSpec(memory_space=pl.ANY),
        out_shape=jax.ShapeDtypeStruct((b, n, d2), F32),
        scratch_shapes=[pltpu.VMEM((2, rows, d2), F32), pltpu.VMEM((3, d, D_EXPERT), BF16),
                        pltpu.SemaphoreType.DMA((2,)), pltpu.SemaphoreType.DMA((2,))],
        input_output_aliases={1: 0},
        compiler_params=pltpu.CompilerParams(dimension_semantics=("arbitrary", "arbitrary"),
                                             vmem_limit_bytes=VMEM_LIMIT_BYTES),
        name="expert_ffn",
    )(idx, xh, gate, mgate, w1, w3, w2)


def _final_norm_kernel(x_ref, w_ref, o_ref):
    x = x_ref[0]
    o_ref[0] = x * lax.rsqrt(jnp.mean(x * x, axis=-1, keepdims=True) + EPS) * w_ref[...]


def _final_norm(x, w, tm):
    b, n, _ = x.shape
    d = D_MODEL
    return pl.pallas_call(
        _final_norm_kernel,
        grid=(b, n // tm),
        in_specs=[pl.BlockSpec((1, tm, d), lambda bi, i: (bi, i, 0)), pl.BlockSpec((1, d), lambda bi, i: (0, 0))],
        out_specs=pl.BlockSpec((1, tm, d), lambda bi, i: (bi, i, 0)),
        out_shape=jax.ShapeDtypeStruct((b, n, d), F32),
        compiler_params=_cparams(("parallel", "parallel")),
        name="final_norm",
    )(x, w)


def _rope_tables(n):
    half = HEAD_DIM // 2
    n_freq = half // 2
    inv = ROPE_THETA ** (-jnp.arange(n_freq, dtype=F32) * 2.0 / half)
    rows = n // GRID_W
    r = jnp.repeat(jnp.arange(rows, dtype=F32), GRID_W)
    col = jnp.tile(jnp.arange(GRID_W, dtype=F32), rows)
    ang = jnp.stack([r[:, None] * inv, col[:, None] * inv], axis=1)
    cos = jnp.cos(ang)[:, :, None, :]
    sin = jnp.sin(ang)[:, :, None, :]
    cos_h = jnp.broadcast_to(cos, (n, 2, 2, n_freq)).reshape(n, HEAD_DIM)
    sin_h = (sin * jnp.array([-1.0, 1.0], F32)[None, None, :, None]).reshape(n, HEAD_DIM)
    return jnp.tile(cos_h, (1, 4)), jnp.tile(sin_h, (1, 4))


def _block_diag(w):
    nb, bi, bj = w.shape
    eye = jnp.eye(nb, dtype=w.dtype)
    return (eye[:, None, :, None] * w[:, :, None, :]).reshape(nb * bi, nb * bj)


def _tile_for(n, pref):
    t = pref
    while n % t:
        t //= 2
    return t


def kernel(x, c, ctx, c_ctx, mod_w, mod_b, norm1_w, w_in, a_conv_w, a_conv_b, a_gate_a_w, a_gate_a_b, a_gate_x_w,
           a_gate_x_b, a_lambda, b_lb_logits, b_onorm_w, c_sink, d_qnorm_w, d_knorm_w, w_out, norm2_w, router_w,
           exp_w1, exp_w3, exp_w2, final_norm_w):
    bsz, n, d = x.shape
    lc = ctx.shape[1]
    depth = mod_w.shape[0]
    cos, sin = _rope_tables(n)
    cos_c = jnp.ones((lc, 256), F32)
    sin_c = jnp.zeros((lc, 256), F32)
    lb_all = jnp.cumsum(jax.nn.softmax(b_lb_logits.astype(F32), axis=0), axis=0)
    lb_all = lb_all - lb_all[0]
    head_ids = jnp.arange(GROUP_W) // HEAD_DIM
    bd_ones = (head_ids[:, None] == head_ids[None, :]).astype(BF16)
    bd_mean = bd_ones * (1.0 / HEAD_DIM)

    s_rows = jnp.concatenate([c, c_ctx[None, :], jnp.zeros((SUBLANES - bsz - 1, d), F32)], axis=0)
    mods = _modulation(s_rows, mod_w, mod_b)

    tm = _tile_for(n, 512)
    tmc = _tile_for(lc, 512)
    ta = _tile_for(n, 512)
    tac = _tile_for(lc, 512)
    cap_l = max(1, CAP_FACTOR * n // N_EXPERTS)
    cap_c = max(1, CAP_FACTOR * lc // N_EXPERTS)

    xl, xc = x, ctx
    for l in range(depth):
        last = l == depth - 1
        m_l = mods[l, :bsz].reshape(bsz, N_MOD, d)
        m_c = jnp.broadcast_to(mods[l, bsz].reshape(1, N_MOD, d), (bsz, N_MOD, d))
        w_in_bf = w_in[l].astype(BF16)
        w_out_bf = w_out[l].astype(BF16)
        qw = jnp.tile(d_qnorm_w[l], 4)[None, :]
        kw = jnp.tile(d_knorm_w[l], 2)[None, :]
        nw1 = norm1_w[l][None, :]
        nw2 = norm2_w[l][None, :]
        conv_w = jnp.concatenate([a_conv_w[l], jnp.zeros((SUBLANES - CONV_W, GROUP_W), F32)], axis=0)
        conv_b = a_conv_b[l][None, :]
        gate_w = jnp.stack([jnp.concatenate([_block_diag(a_gate_a_w[l, dd]), _block_diag(a_gate_x_w[l, dd])], axis=1)
                            for dd in range(2)]).astype(BF16)
        gate_b = jnp.concatenate([a_gate_a_b[l], a_gate_x_b[l]], axis=1)[:, None, :]
        lam = a_lambda[l][:, None, :]
        lb = lb_all[l][None, :]
        onorm = jnp.tile(b_onorm_w[l], B_HEADS)[None, :]
        router_t = jnp.transpose(router_w[l])
        sink = c_sink[l].astype(F32)

        za_c, zb_c, qc_c, qd_c = _inproj(xc, nw1, m_c[:, 0:1], m_c[:, 1:2], w_in_bf, cos_c, sin_c, qw, kw, bd_mean, tmc)
        za_l, zb_l, qc_l, qd_l = _inproj(xl, nw1, m_l[:, 0:1], m_l[:, 1:2], w_in_bf, cos, sin, qw, kw, bd_mean, tm)

        h0 = jnp.zeros((bsz, SUBLANES, GROUP_W), F32)
        hf_c, hb_c, hlast = _lru(za_c, h0, conv_w, conv_b, gate_w, gate_b, lam, tac)
        hf_l, hb_l, _ = _lru(za_l, hlast, conv_w, conv_b, gate_w, gate_b, lam, ta)

        s0 = jnp.zeros((bsz, 2, GROUP_W, GROUP_W), F32)
        of_c, ob_c, slast = _hgrn(zb_c, s0, lb, bd_ones)
        of_l, ob_l, _ = _hgrn(zb_l, slast, lb, bd_ones)

        yc_l = _window_attention(qc_l, qc_c, sink, _tile_for(n, 256))
        yd_l = _dense_attention(qd_l, qd_c, _tile_for(n, 512), _tile_for(n, 2048), 256)

        def after_mixer(hf, hb, za, of, ob, zb, yc, yd, xin, mm, tile, cap):
            mrows = jnp.concatenate([mm[:, 2:5], jnp.zeros((bsz, SUBLANES - 3, d), F32)], axis=1)
            xh, aff_t = _outproj(hf, hb, za, of, ob, zb, yc, yd, xin, w_out_bf, mrows, nw2, onorm, bd_mean,
                                 router_t, tile)
            idx, gate = _route(aff_t, xin.shape[1], cap)
            idx = idx.reshape(bsz * N_EXPERTS, 1, cap)
            gate = gate.reshape(bsz, N_EXPERTS, cap, 1)
            return _expert_ffn(idx, gate, xh, mm[:, 5:6], exp_w1, exp_w3, exp_w2, l)

        xl = after_mixer(hf_l, hb_l, za_l, of_l, ob_l, zb_l, yc_l, yd_l, xl, m_l, tm, cap_l)
        if not last:
            yc_c = _ctx_attention(qc_c, sink, True)
            yd_c = _ctx_attention(qd_c, sink, False)
            xc = after_mixer(hf_c, hb_c, za_c, of_c, ob_c, zb_c, yc_c, yd_c, xc, m_c, tmc, cap_c)
    return _final_norm(xl, final_norm_w[None, :], tm)
```

```python
import functools
import math

import jax
import jax.numpy as jnp
import numpy as np
from jax import lax
from jax.experimental import pallas as pl
from jax.experimental.pallas import tpu as pltpu

D_MODEL = 1024
GRID_W = 64
GROUP_W = D_MODEL // 4
HEAD_DIM = 64
EPS = 1e-6
A_BLOCKS = 4
CONV_W = 4
LRU_C = 8.0
B_HEADS = 4
WINDOW = 128
ROPE_THETA = 10000.0
N_EXPERTS = 16
CAP_FACTOR = 2
D_EXPERT = 1024
D_IN = 2816
N_MOD = 6

LANES = 128
SUBLANES = 8
VMEM_LIMIT_BYTES = 56 * 1024 * 1024

NEG_BIG = -1e30
LOG2_E = math.log2(math.e)
F32 = jnp.float32
BF16 = jnp.bfloat16


def _cparams(sem):
    return pltpu.CompilerParams(dimension_semantics=sem, vmem_limit_bytes=VMEM_LIMIT_BYTES)


def _dot(a, b):
    return jnp.dot(a, b, preferred_element_type=F32)


def _dot_nt(a, b):
    return lax.dot_general(a, b, (((1,), (1,)), ((), ())), preferred_element_type=F32)


def _silu(x):
    return x * (1.0 / (1.0 + jnp.exp(-x)))


def _sigmoid(x):
    return 1.0 / (1.0 + jnp.exp(-x))


def _iota(shape, dim):
    return lax.broadcasted_iota(jnp.int32, shape, dim)


def _mod_kernel(s_ref, w_ref, b_ref, o_ref):
    s = _silu(s_ref[...])
    o_ref[0] = jnp.dot(s, w_ref[0], precision=lax.Precision.HIGHEST, preferred_element_type=F32) + b_ref[0]


def _modulation(s_rows, mod_w, mod_b):
    depth, d, n = mod_w.shape
    tn = 1536
    return pl.pallas_call(
        _mod_kernel,
        grid=(depth, n // tn),
        in_specs=[pl.BlockSpec((SUBLANES, d), lambda l, j: (0, 0)),
                  pl.BlockSpec((1, d, tn), lambda l, j: (l, 0, j)),
                  pl.BlockSpec((1, 1, tn), lambda l, j: (l, 0, j))],
        out_specs=pl.BlockSpec((1, SUBLANES, tn), lambda l, j: (l, 0, j)),
        out_shape=jax.ShapeDtypeStruct((depth, SUBLANES, n), F32),
        compiler_params=_cparams(("parallel", "parallel")),
        name="modulation",
    )(s_rows, mod_w, mod_b.reshape(depth, 1, n))


def _rope(x, cos, sin_signed):
    w = x.shape[-1]
    lane = _iota(x.shape, 1)
    partner = jnp.where((lane % 32) < 16, pltpu.roll(x, w - 16, 1), pltpu.roll(x, 16, 1))
    return x * cos + partner * sin_signed


def _head_rms(x, bd_mean, w):
    ms = _dot((x * x).astype(BF16), bd_mean)
    return x * lax.rsqrt(ms + EPS) * w


def _pack_qkv(o_ref, q, k, v):
    o_ref[0, :, 0:256] = q.astype(BF16)
    o_ref[0, :, 256:384] = k.astype(BF16)
    o_ref[0, :, 384:512] = pltpu.roll(k, 64, 1).astype(BF16)
    o_ref[0, :, 512:640] = v.astype(BF16)
    o_ref[0, :, 640:768] = pltpu.roll(v, 64, 1).astype(BF16)


def _inproj_kernel(x_ref, nw_ref, sh_ref, sc_ref, w_ref, cos_ref, sin_ref, qw_ref, kw_ref, bd_ref,
                   za_ref, zb_ref, qc_ref, qd_ref):
    x = x_ref[0]
    y = x * lax.rsqrt(jnp.mean(x * x, axis=-1, keepdims=True) + EPS) * nw_ref[...]
    h = y * (1.0 + sc_ref[0]) + sh_ref[0]
    z = _dot(h.astype(BF16), w_ref[...])
    za_ref[0] = z[:, 0:512]
    zb_ref[0] = z[:, 512:1792]
    cos = cos_ref[...]
    sin = sin_ref[...]
    scale = HEAD_DIM ** -0.5
    cq = _rope(z[:, 1792:2048], cos, sin) * scale
    ck = _rope(z[:, 2048:2176], cos[:, 0:128], sin[:, 0:128])
    _pack_qkv(qc_ref, cq, ck, z[:, 2176:2304])
    bd = bd_ref[...]
    dq = _rope(_head_rms(z[:, 2304:2560], bd, qw_ref[...]), cos, sin) * (scale * LOG2_E)
    dk = _rope(_head_rms(z[:, 2560:2688], bd[0:128, 0:128], kw_ref[...]), cos[:, 0:128], sin[:, 0:128])
    _pack_qkv(qd_ref, dq, dk, z[:, 2688:2816])


def _inproj(x, norm_w, shift, scale, w_in_bf, cos, sin, qw, kw, bd_mean, tm):
    b, n, _ = x.shape
    d = D_MODEL
    row = lambda bi, i: (bi, i, 0)
    const2 = lambda bi, i: (0, 0)
    return pl.pallas_call(
        _inproj_kernel,
        grid=(b, n // tm),
        in_specs=[pl.BlockSpec((1, tm, d), row),
                  pl.BlockSpec((1, d), const2),
                  pl.BlockSpec((1, 1, d), lambda bi, i: (bi, 0, 0)),
                  pl.BlockSpec((1, 1, d), lambda bi, i: (bi, 0, 0)),
                  pl.BlockSpec((d, D_IN), const2),
                  pl.BlockSpec((tm, 256), lambda bi, i: (i, 0)),
                  pl.BlockSpec((tm, 256), lambda bi, i: (i, 0)),
                  pl.BlockSpec((1, 256), const2),
                  pl.BlockSpec((1, 128), const2),
                  pl.BlockSpec((256, 256), const2)],
        out_specs=[pl.BlockSpec((1, tm, 512), row), pl.BlockSpec((1, tm, 1280), row),
                   pl.BlockSpec((1, tm, 768), row), pl.BlockSpec((1, tm, 768), row)],
        out_shape=[jax.ShapeDtypeStruct((b, n, 512), F32), jax.ShapeDtypeStruct((b, n, 1280), F32),
                   jax.ShapeDtypeStruct((b, n, 768), BF16), jax.ShapeDtypeStruct((b, n, 768), BF16)],
        compiler_params=_cparams(("parallel", "parallel")),
        name="inproj",
    )(x, norm_w, shift, scale, w_in_bf, cos, sin, qw, kw, bd_mean)


def _lru_kernel(fc_ref, fp_ref, fn_ref, bc_ref, bp_ref, bn_ref, h0_ref, cw_ref, cb_ref, gw_ref, gb_ref, lam_ref,
                hf_ref, hb_ref, hl_ref, carry_ref, *, t, nt):
    i = pl.program_id(1)

    @pl.when(i == 0)
    def _():
        carry_ref[...] = h0_ref[0]

    row = _iota((t, GROUP_W), 0)
    cw = cw_ref[...]

    def coeffs(cur_ref, prev_ref, next_ref, tile, d):
        prev = jnp.where(tile == 0, 0.0, prev_ref[0])
        nxt = jnp.where(tile == nt - 1, 0.0, next_ref[0])
        ext = jnp.concatenate([prev, cur_ref[0], nxt], axis=0)
        n_ext = t + 2 * SUBLANES
        xa = cb_ref[...] + cw[2:3] * ext[SUBLANES:SUBLANES + t]
        for j, off in ((0, 2), (1, 1), (3, -1)):
            xa = xa + cw[j:j + 1] * pltpu.roll(ext, off % n_ext, 0)[SUBLANES:SUBLANES + t]
        g = _dot(xa.astype(BF16), gw_ref[d]) + gb_ref[d]
        r = _sigmoid(g[:, 0:GROUP_W])
        gi = _sigmoid(g[:, GROUP_W:2 * GROUP_W])
        lam = lam_ref[d]
        softplus = jnp.maximum(-lam, 0.0) + jnp.log(1.0 + jnp.exp(-jnp.abs(lam)))
        log_a = (-LRU_C) * r * softplus
        a = jnp.exp(log_a)
        u = jnp.sqrt(1.0 - jnp.exp(2.0 * log_a)) * (gi * xa)
        return a, u

    a, u = coeffs(fc_ref, fp_ref, fn_ref, i, 0)
    s = 1
    while s < t:
        keep = row >= s
        a_sh = jnp.where(keep, pltpu.roll(a, s, 0), 1.0)
        u_sh = jnp.where(keep, pltpu.roll(u, s, 0), 0.0)
        u = a * u_sh + u
        a = a * a_sh
        s *= 2
    h = u + a * carry_ref[0:1]
    hf_ref[0] = h
    carry_ref[0:1] = hf_ref[0, t - 1:t, :]

    a, u = coeffs(bc_ref, bp_ref, bn_ref, nt - 1 - i, 1)
    s = 1
    while s < t:
        keep = row < t - s
        a_sh = jnp.where(keep, pltpu.roll(a, t - s, 0), 1.0)
        u_sh = jnp.where(keep, pltpu.roll(u, t - s, 0), 0.0)
        u = a * u_sh + u
        a = a * a_sh
        s *= 2
    h = u + a * carry_ref[1:2]
    hb_ref[0] = h
    carry_ref[1:2] = hb_ref[0, 0:1, :]

    @pl.when(i == nt - 1)
    def _():
        hl_ref[0] = carry_ref[...]


def _lru(za, h0, conv_w, conv_b, gate_w, gate_b, lam, t):
    b, n, _ = za.shape
    nt = n // t
    t8 = t // SUBLANES
    n8 = n // SUBLANES
    cur_f = lambda bi, i: (bi, i, 0)
    prev_f = lambda bi, i: (bi, jnp.maximum(i * t8 - 1, 0), 0)
    next_f = lambda bi, i: (bi, jnp.minimum((i + 1) * t8, n8 - 1), 0)
    cur_b = lambda bi, i: (bi, nt - 1 - i, 0)
    prev_b = lambda bi, i: (bi, jnp.maximum((nt - 1 - i) * t8 - 1, 0), 0)
    next_b = lambda bi, i: (bi, jnp.minimum((nt - i) * t8, n8 - 1), 0)
    c2 = lambda bi, i: (0, 0)
    c3 = lambda bi, i: (0, 0, 0)
    halo = (1, SUBLANES, GROUP_W)
    return pl.pallas_call(
        functools.partial(_lru_kernel, t=t, nt=nt),
        grid=(b, nt),
        in_specs=[pl.BlockSpec((1, t, GROUP_W), cur_f), pl.BlockSpec(halo, prev_f), pl.BlockSpec(halo, next_f),
                  pl.BlockSpec((1, t, GROUP_W), cur_b), pl.BlockSpec(halo, prev_b), pl.BlockSpec(halo, next_b),
                  pl.BlockSpec((1, SUBLANES, GROUP_W), lambda bi, i: (bi, 0, 0)),
                  pl.BlockSpec((SUBLANES, GROUP_W), c2), pl.BlockSpec((1, GROUP_W), c2),
                  pl.BlockSpec((2, GROUP_W, 2 * GROUP_W), c3), pl.BlockSpec((2, 1, 2 * GROUP_W), c3),
                  pl.BlockSpec((2, 1, GROUP_W), c3)],
        out_specs=[pl.BlockSpec((1, t, GROUP_W), cur_f), pl.BlockSpec((1, t, GROUP_W), cur_b),
                   pl.BlockSpec((1, SUBLANES, GROUP_W), lambda bi, i: (bi, 0, 0))],
        out_shape=[jax.ShapeDtypeStruct((b, n, GROUP_W), F32), jax.ShapeDtypeStruct((b, n, GROUP_W), F32),
                   jax.ShapeDtypeStruct((b, SUBLANES, GROUP_W), F32)],
        scratch_shapes=[pltpu.VMEM((SUBLANES, GROUP_W), F32)],
        compiler_params=_cparams(("parallel", "arbitrary")),
        name="rglru",
    )(za, za, za, za, za, za, h0, conv_w, conv_b, gate_w, gate_b, lam)


B_CHUNK = 128
B_SUB = 32


def _hgrn_chunk(q_raw, f_raw, v, lb, st_ref, d, b_ref, k_ref, v_ref, bd_ones, reverse):
    t = B_CHUNK
    n_sub = t // B_SUB
    q = _silu(q_raw) * (GROUP_W // B_HEADS) ** -0.5
    log_sig = jnp.minimum(f_raw, 0.0) - jnp.log(1.0 + jnp.exp(-jnp.abs(f_raw)))
    log_lb = jnp.log(lb)
    y = jnp.log(1.0 - lb) + log_sig
    log_f = jnp.maximum(log_lb, y) + jnp.log(1.0 + jnp.exp(-jnp.abs(log_lb - y)))
    k = (1.0 - lb) * _sigmoid(-f_raw)
    row = _iota((t, GROUP_W), 0)
    b = log_f * LOG2_E
    s = 1
    while s < t:
        if reverse:
            b = b + jnp.where(row < t - s, pltpu.roll(b, t - s, 0), 0.0)
        else:
            b = b + jnp.where(row >= s, pltpu.roll(b, s, 0), 0.0)
        s *= 2
    b_ref[...] = b
    k_ref[...] = b - jnp.log2(k)
    v_ref[...] = v
    edge = 0 if reverse else t - 1
    b_edge = b_ref[edge:edge + 1, :]
    st = st_ref[d]
    lane = _iota((B_SUB, GROUP_W), 1)
    head_masks = [(lane >= 64 * hh) & (lane < 64 * (hh + 1)) for hh in range(B_HEADS)]
    trow = _iota((B_SUB, GROUP_W), 0)
    srow = _iota((t, GROUP_W), 0)

    o_inter = _dot_nt((q * jnp.exp2(b)).astype(BF16), st.astype(BF16))
    outs = []
    for i in range(n_sub):
        lo = i * B_SUB
        qb = q[lo:lo + B_SUB]
        bb = b[lo:lo + B_SUB]
        acc = o_inter[lo:lo + B_SUB]
        has_off = (i < n_sub - 1) if reverse else (i > 0)
        if has_off:
            ref_row = lo + B_SUB if reverse else lo - 1
            r_i = b_ref[ref_row:ref_row + 1, :]
            qh = qb * jnp.exp2(bb - r_i)
            key_ok = (srow >= lo + B_SUB) if reverse else (srow < lo)
            kh = jnp.where(key_ok, k * jnp.exp2(jnp.minimum(r_i - b, 0.0)), 0.0)
            qstack = jnp.concatenate([jnp.where(m, qh, 0.0) for m in head_masks], axis=0).astype(BF16)
            att = _dot_nt(qstack, kh.astype(BF16))
            res = _dot(att.astype(BF16), v.astype(BF16))
            for hh in range(B_HEADS):
                acc = acc + jnp.where(head_masks[hh], res[hh * B_SUB:(hh + 1) * B_SUB], 0.0)
        pieces = []
        for sl in range(B_SUB):
            level = k_ref[lo + sl:lo + sl + 1, :]
            ok = (trow <= sl) if reverse else (trow >= sl)
            e = jnp.exp2(jnp.where(ok, bb - level, NEG_BIG)) * qb
            pieces.append(e.astype(BF16))
        g = _dot(jnp.concatenate(pieces, axis=0), bd_ones)
        for sl in range(B_SUB):
            acc = acc + g[sl * B_SUB:(sl + 1) * B_SUB] * v_ref[lo + sl:lo + sl + 1, :]
        outs.append(acc)
    kt = k * jnp.exp2(b_edge - b)
    upd = _dot(jnp.transpose(v).astype(BF16), kt.astype(BF16))
    vrow = _iota((GROUP_W, GROUP_W), 0) // 64
    kcol = _iota((GROUP_W, GROUP_W), 1) // 64
    st_ref[d] = st * jnp.exp2(b_edge) + jnp.where(vrow == kcol, upd, 0.0)
    return jnp.concatenate(outs, axis=0)


def _hgrn_kernel(qf_ref, ff_ref, vf_ref, qb_ref, fb_ref, vb_ref, s0_ref, lb_ref, bd_ref,
                 of_ref, ob_ref, sl_ref, st_ref, b_ref, k_ref, v_ref, *, nt):
    i = pl.program_id(1)

    @pl.when(i == 0)
    def _():
        st_ref[...] = s0_ref[0]

    lb = lb_ref[...]
    bd = bd_ref[...]
    of_ref[0] = _hgrn_chunk(qf_ref[0], ff_ref[0], vf_ref[0], lb, st_ref, 0, b_ref, k_ref, v_ref, bd, False)
    ob_ref[0] = _hgrn_chunk(qb_ref[0], fb_ref[0], vb_ref[0], lb, st_ref, 1, b_ref, k_ref, v_ref, bd, True)

    @pl.when(i == nt - 1)
    def _():
        sl_ref[0] = st_ref[...]


def _hgrn(zb, s0, lb, bd_ones):
    b, n, _ = zb.shape
    t = B_CHUNK
    nt = n // t
    blk = (1, t, GROUP_W)
    fwd = lambda col: (lambda bi, i: (bi, i, col))
    bwd = lambda col: (lambda bi, i: (bi, nt - 1 - i, col))
    state_spec = pl.BlockSpec((1, 2, GROUP_W, GROUP_W), lambda bi, i: (bi, 0, 0, 0))
    return pl.pallas_call(
        functools.partial(_hgrn_kernel, nt=nt),
        grid=(b, nt),
        in_specs=[pl.BlockSpec(blk, fwd(0)), pl.BlockSpec(blk, fwd(1)), pl.BlockSpec(blk, fwd(3)),
                  pl.BlockSpec(blk, bwd(0)), pl.BlockSpec(blk, bwd(2)), pl.BlockSpec(blk, bwd(3)),
                  state_spec,
                  pl.BlockSpec((1, GROUP_W), lambda bi, i: (0, 0)),
                  pl.BlockSpec((GROUP_W, GROUP_W), lambda bi, i: (0, 0))],
        out_specs=[pl.BlockSpec(blk, fwd(0)), pl.BlockSpec(blk, bwd(0)), state_spec],
        out_shape=[jax.ShapeDtypeStruct((b, n, GROUP_W), F32), jax.ShapeDtypeStruct((b, n, GROUP_W), F32),
                   jax.ShapeDtypeStruct((b, 2, GROUP_W, GROUP_W), F32)],
        scratch_shapes=[pltpu.VMEM((2, GROUP_W, GROUP_W), F32), pltpu.VMEM((t, GROUP_W), F32),
                        pltpu.VMEM((t, GROUP_W), F32), pltpu.VMEM((t, GROUP_W), F32)],
        compiler_params=_cparams(("parallel", "arbitrary")),
        name="hgrn2",
    )(zb, zb, zb, zb, zb, zb, s0, lb, bd_ones)


def _stack_q(q):
    tq = q.shape[0]
    lane = _iota((tq, LANES), 1)
    lo = lane < 64
    q0 = q[:, 0:128]
    q1 = q[:, 128:256]
    zero = jnp.zeros_like(q0)
    s1 = jnp.concatenate([jnp.where(lo, q0, zero), jnp.where(lo, zero, q1)], axis=0)
    s2 = jnp.concatenate([jnp.where(lo, zero, q0), jnp.where(lo, q1, zero)], axis=0)
    return s1, s2


def _unstack_out(r1, r2, tq):
    lane = _iota((tq, LANES), 1)
    lo = lane < 64
    o0 = jnp.where(lo, r1[0:tq], r2[0:tq])
    o1 = jnp.where(lo, r2[tq:2 * tq], r1[tq:2 * tq])
    return jnp.concatenate([o0, o1], axis=1)


def _sink_cols(sink_ref, tq):
    row = _iota((2 * tq, 1), 0)
    c1 = jnp.where(row < tq, sink_ref[0], sink_ref[3])
    c2 = jnp.where(row < tq, sink_ref[1], sink_ref[2])
    return c1, c2


def _window_kernel(sink_ref, q_ref, kp_ref, kc_ref, kn_ref, vp_ref, vc_ref, vn_ref, kx_ref, vx_ref, o_ref,
                   *, tq, n):
    i = pl.program_id(1)
    s1, s2 = _stack_q(q_ref[0])
    kband = jnp.concatenate([kp_ref[0], kc_ref[0], kn_ref[0]], axis=0)
    vband = jnp.concatenate([vp_ref[0], vc_ref[0], vn_ref[0]], axis=0)
    nband = tq + 2 * WINDOW
    qpos = i * tq + (_iota((2 * tq, nband), 0) % tq)
    kpos = i * tq - WINDOW + _iota((2 * tq, nband), 1)
    ok = (kpos >= 0) & (kpos < n) & (jnp.abs(qpos - kpos) <= WINDOW)
    sinks = _sink_cols(sink_ref, tq)
    res = []
    for qs, half, sink in ((s1, 0, sinks[0]), (s2, 1, sinks[1])):
        kx = kx_ref[0][:, half * 128:(half + 1) * 128]
        vx = vx_ref[0][:, half * 128:(half + 1) * 128]
        sc = _dot_nt(qs, kx)
        sw = jnp.where(ok, _dot_nt(qs, kband[:, half * 128:(half + 1) * 128]), NEG_BIG)
        m = jnp.maximum(jnp.maximum(jnp.max(sc, axis=-1, keepdims=True), jnp.max(sw, axis=-1, keepdims=True)), sink)
        pc = jnp.exp(sc - m)
        pw = jnp.exp(sw - m)
        den = jnp.sum(pc, axis=-1, keepdims=True) + jnp.sum(pw, axis=-1, keepdims=True) + jnp.exp(sink - m)
        num = _dot(pc.astype(BF16), vx) + _dot(pw.astype(BF16), vband[:, half * 128:(half + 1) * 128])
        res.append(num / den)
    o_ref[0] = _unstack_out(res[0], res[1], tq).astype(BF16)


def _window_attention(qkv, qkv_ctx, sink, tq):
    b, n, _ = qkv.shape
    lc = qkv_ctx.shape[1]
    w = WINDOW
    r = tq // w
    nw = n // w
    prev = lambda col: (lambda bi, i, s: (bi, jnp.maximum(i * r - 1, 0), col))
    cur = lambda col: (lambda bi, i, s: (bi, i, col))
    nxt = lambda col: (lambda bi, i, s: (bi, jnp.minimum((i + 1) * r, nw - 1), col))
    ctx = lambda col: (lambda bi, i, s: (bi, 0, col))
    grid_spec = pltpu.PrefetchScalarGridSpec(
        num_scalar_prefetch=1,
        grid=(b, n // tq),
        in_specs=[pl.BlockSpec((1, tq, 256), cur(0)),
                  pl.BlockSpec((1, w, 256), prev(1)), pl.BlockSpec((1, tq, 256), cur(1)),
                  pl.BlockSpec((1, w, 256), nxt(1)),
                  pl.BlockSpec((1, w, 256), prev(2)), pl.BlockSpec((1, tq, 256), cur(2)),
                  pl.BlockSpec((1, w, 256), nxt(2)),
                  pl.BlockSpec((1, lc, 256), ctx(1)), pl.BlockSpec((1, lc, 256), ctx(2))],
        out_specs=pl.BlockSpec((1, tq, 256), cur(0)),
    )
    return pl.pallas_call(
        functools.partial(_window_kernel, tq=tq, n=n),
        grid_spec=grid_spec,
        out_shape=jax.ShapeDtypeStruct((b, n, 256), BF16),
        compiler_params=_cparams(("parallel", "parallel")),
        name="window_attn",
    )(sink, qkv, qkv, qkv, qkv, qkv, qkv, qkv, qkv_ctx, qkv_ctx)


def _ctx_attn_kernel(sink_ref, q_ref, k_ref, v_ref, o_ref, *, tq, use_sink):
    s1, s2 = _stack_q(q_ref[0])
    sinks = _sink_cols(sink_ref, tq)
    res = []
    for qs, half, sink in ((s1, 0, sinks[0]), (s2, 1, sinks[1])):
        kx = k_ref[0][:, half * 128:(half + 1) * 128]
        vx = v_ref[0][:, half * 128:(half + 1) * 128]
        sc = _dot_nt(qs, kx)
        m = jnp.max(sc, axis=-1, keepdims=True)
        if use_sink:
            m = jnp.maximum(m, sink)
        p = jnp.exp(sc - m) if use_sink else jnp.exp2(sc - m)
        den = jnp.sum(p, axis=-1, keepdims=True)
        if use_sink:
            den = den + jnp.exp(sink - m)
        res.append(_dot(p.astype(BF16), vx) / den)
    o_ref[0] = _unstack_out(res[0], res[1], tq).astype(BF16)


def _ctx_attention(qkv_ctx, sink, use_sink):
    b, lc, _ = qkv_ctx.shape
    col = lambda c: (lambda bi, s: (bi, 0, c))
    grid_spec = pltpu.PrefetchScalarGridSpec(
        num_scalar_prefetch=1,
        grid=(b,),
        in_specs=[pl.BlockSpec((1, lc, 256), col(0)), pl.BlockSpec((1, lc, 256), col(1)),
                  pl.BlockSpec((1, lc, 256), col(2))],
        out_specs=pl.BlockSpec((1, lc, 256), col(0)),
    )
    return pl.pallas_call(
        functools.partial(_ctx_attn_kernel, tq=lc, use_sink=use_sink),
        grid_spec=grid_spec,
        out_shape=jax.ShapeDtypeStruct((b, lc, 256), BF16),
        compiler_params=_cparams(("parallel",)),
        name="ctx_attn",
    )(sink, qkv_ctx, qkv_ctx, qkv_ctx)


def _dense_kernel(q_ref, k_ref, v_ref, kx_ref, vx_ref, o_ref, qs_ref, m_ref, acc_ref, *, tq, tk, ck, rb, nk):
    ki = pl.program_id(2)
    rows = 2 * tq

    def update(half, kc, vc):
        n = kc.shape[0]
        vext = jnp.concatenate([vc, jnp.ones((n, LANES), BF16)], axis=1)
        for r in range(rows // rb):
            sl = slice(r * rb, (r + 1) * rb)
            s = _dot_nt(qs_ref[half, sl, :], kc)
            m_old = m_ref[half, sl, :]
            m_new = jnp.maximum(m_old, jnp.max(s, axis=-1, keepdims=True))
            alpha = jnp.exp2(m_old - m_new)
            p = jnp.exp2(s - jnp.tile(m_new, (1, n // LANES))).astype(BF16)
            m_ref[half, sl, :] = m_new
            acc_ref[half, sl, :] = acc_ref[half, sl, :] * jnp.tile(alpha, (1, 2)) + _dot(p, vext)

    @pl.when(ki == 0)
    def _():
        s1, s2 = _stack_q(q_ref[0])
        qs_ref[0] = s1
        qs_ref[1] = s2
        m_ref[...] = jnp.full(m_ref.shape, NEG_BIG, F32)
        acc_ref[...] = jnp.zeros(acc_ref.shape, F32)
        for half in range(2):
            update(half, kx_ref[0][:, half * 128:(half + 1) * 128], vx_ref[0][:, half * 128:(half + 1) * 128])

    for c in range(tk // ck):
        for half in range(2):
            update(half, k_ref[0, c * ck:(c + 1) * ck, half * 128:(half + 1) * 128],
                   v_ref[0, c * ck:(c + 1) * ck, half * 128:(half + 1) * 128])

    @pl.when(ki == nk - 1)
    def _():
        a1 = acc_ref[0]
        a2 = acc_ref[1]
        r1 = a1[:, 0:LANES] / a1[:, LANES:2 * LANES]
        r2 = a2[:, 0:LANES] / a2[:, LANES:2 * LANES]
        o_ref[0] = _unstack_out(r1, r2, tq).astype(BF16)


def _dense_attention(qkv, qkv_ctx, tq, tk, ck):
    b, n, _ = qkv.shape
    lc = qkv_ctx.shape[1]
    nk = n // tk
    rb = min(2 * tq, 256)
    return pl.pallas_call(
        functools.partial(_dense_kernel, tq=tq, tk=tk, ck=ck, rb=rb, nk=nk),
        grid=(b, n // tq, nk),
        in_specs=[pl.BlockSpec((1, tq, 256), lambda bi, qi, ki: (bi, qi, 0)),
                  pl.BlockSpec((1, tk, 256), lambda bi, qi, ki: (bi, ki, 1)),
                  pl.BlockSpec((1, tk, 256), lambda bi, qi, ki: (bi, ki, 2)),
                  pl.BlockSpec((1, lc, 256), lambda bi, qi, ki: (bi, 0, 1)),
                  pl.BlockSpec((1, lc, 256), lambda bi, qi, ki: (bi, 0, 2))],
        out_specs=pl.BlockSpec((1, tq, 256), lambda bi, qi, ki: (bi, qi, 0)),
        out_shape=jax.ShapeDtypeStruct((b, n, 256), BF16),
        scratch_shapes=[pltpu.VMEM((2, 2 * tq, LANES), BF16), pltpu.VMEM((2, 2 * tq, LANES), F32),
                        pltpu.VMEM((2, 2 * tq, 2 * LANES), F32)],
        compiler_params=_cparams(("parallel", "parallel", "arbitrary")),
        name="dense_attn",
    )(qkv, qkv, qkv, qkv_ctx, qkv_ctx)


def _gelu_tanh(x):
    return 0.5 * x * (1.0 + jnp.tanh(math.sqrt(2.0 / math.pi) * (x + 0.044715 * (x * x * x))))


def _outproj_kernel(hf_ref, hb_ref, ay_ref, of_ref, ob_ref, bg_ref, yc_ref, yd_ref, x_ref, w_ref, m_ref, nw_ref,
                    ow_ref, bd_ref, rw_ref, xh_ref, aff_ref):
    ya = (hf_ref[0] + hb_ref[0]) * _gelu_tanh(ay_ref[0])
    o = of_ref[0] + ob_ref[0]
    ms = _dot((o * o).astype(BF16), bd_ref[...])
    yb = o * lax.rsqrt(ms + EPS) * ow_ref[...] * _silu(bg_ref[0])
    y = jnp.concatenate([ya.astype(BF16), yb.astype(BF16), yc_ref[0], yd_ref[0]], axis=1)
    out = _dot(y, w_ref[...])
    m = m_ref[0]
    x = x_ref[0] + m[0:1] * out
    hn = x * lax.rsqrt(jnp.mean(x * x, axis=-1, keepdims=True) + EPS) * nw_ref[...]
    h2 = hn * (1.0 + m[2:3]) + m[1:2]
    xh_ref[0, :, 0:D_MODEL] = x
    xh_ref[0, :, D_MODEL:2 * D_MODEL] = h2
    logits = lax.dot_general(rw_ref[...], h2, (((1,), (1,)), ((), ())), precision=lax.Precision.HIGHEST,
                             preferred_element_type=F32)
    mx = jnp.max(logits, axis=0, keepdims=True)
    e = jnp.exp(logits - mx)
    aff_ref[0] = e / jnp.sum(e, axis=0, keepdims=True)


def _outproj(hf, hb, za, of, ob, zb, yc, yd, x, w_out_bf, mrows, norm2_w, onorm_w, bd_mean, router_t, tm):
    b, n, _ = x.shape
    d = D_MODEL
    row = lambda col: (lambda bi, i: (bi, i, col))
    c2 = lambda bi, i: (0, 0)
    g = (1, tm, GROUP_W)
    return pl.pallas_call(
        _outproj_kernel,
        grid=(b, n // tm),
        in_specs=[pl.BlockSpec(g, row(0)), pl.BlockSpec(g, row(0)), pl.BlockSpec(g, row(1)),
                  pl.BlockSpec(g, row(0)), pl.BlockSpec(g, row(0)), pl.BlockSpec(g, row(4)),
                  pl.BlockSpec(g, row(0)), pl.BlockSpec(g, row(0)),
                  pl.BlockSpec((1, tm, d), row(0)),
                  pl.BlockSpec((d, d), c2),
                  pl.BlockSpec((1, SUBLANES, d), lambda bi, i: (bi, 0, 0)),
                  pl.BlockSpec((1, d), c2), pl.BlockSpec((1, GROUP_W), c2),
                  pl.BlockSpec((GROUP_W, GROUP_W), c2), pl.BlockSpec((N_EXPERTS, d), c2)],
        out_specs=[pl.BlockSpec((1, tm, 2 * d), row(0)),
                   pl.BlockSpec((1, N_EXPERTS, tm), lambda bi, i: (bi, 0, i))],
        out_shape=[jax.ShapeDtypeStruct((b, n, 2 * d), F32),
                   jax.ShapeDtypeStruct((b, N_EXPERTS, n), F32)],
        compiler_params=_cparams(("parallel", "parallel")),
        name="outproj",
    )(hf, hb, za, of, ob, zb, yc, yd, x, w_out_bf, mrows, norm2_w, onorm_w, bd_mean, router_t)


def _topk_kernel(aff_ref, idx_ref, gate_ref, *, gp, cap):
    x_all = aff_ref[0]
    xb_all = pltpu.bitcast(x_all, jnp.int32)
    kf = float(cap)

    lo = jnp.zeros((N_EXPERTS, 1, 1), jnp.int32)
    hi = jnp.full((N_EXPERTS, 1, 1), 0x7F800000, jnp.int32)

    def bisect(_, carry):
        lo, hi = carry
        mid = lo + lax.shift_right_logical(hi - lo, 1)
        cnt = jnp.sum(jnp.sum(jnp.where(xb_all >= mid, 1.0, 0.0), axis=2, keepdims=True), axis=1, keepdims=True)
        ge = cnt >= kf
        return jnp.where(ge, mid, lo), jnp.where(ge, hi, mid)

    lo, hi = lax.fori_loop(0, 31, bisect, (lo, hi))

    upper = jnp.where(_iota((LANES, LANES), 0) <= _iota((LANES, LANES), 1), 1.0, 0.0).astype(BF16)
    strict = jnp.where(_iota((gp, gp), 1) < _iota((gp, gp), 0), 1.0, 0.0).astype(BF16)
    g_col = _iota((gp, 1), 0).astype(F32)
    j_col = _iota((LANES, 1), 0).astype(F32)
    slot = _iota((1, cap), 1).astype(F32)

    def prefix(mask_f):
        within = _dot(mask_f.astype(BF16), upper)
        before = jnp.sum(_dot(strict, mask_f.astype(BF16)), axis=1, keepdims=True)
        return within, before

    for e in range(N_EXPERTS):
        x = x_all[e]
        xb = xb_all[e]
        thr = lo[e]
        gt = jnp.where(xb > thr, 1.0, 0.0)
        eq = jnp.where(xb == thr, 1.0, 0.0)
        need = kf - jnp.sum(jnp.sum(gt, axis=1, keepdims=True), axis=0, keepdims=True)
        eq_within, eq_before = prefix(eq)
        tie_rank = eq_before + eq_within - eq
        sel = jnp.maximum(gt, jnp.where(tie_rank < need, eq, 0.0))
        within, before = prefix(sel)
        incl = before + within[:, LANES - 1:LANES]
        g_of_s = jnp.sum(jnp.where(incl <= slot, 1.0, 0.0), axis=0, keepdims=True)
        base = jnp.max(jnp.where(before <= slot, before, 0.0), axis=0, keepdims=True)
        rank = slot - base
        onehot = jnp.where(g_col == g_of_s, 1.0, 0.0).astype(BF16)
        counts = _dot(jnp.transpose(within).astype(BF16), onehot)
        j_of_s = jnp.sum(jnp.where(counts <= rank, 1.0, 0.0), axis=0, keepdims=True)
        idx_ref[0, e] = (g_of_s * float(LANES) + j_of_s).astype(jnp.int32)
        xt = jnp.transpose(x)
        x_hi = xt.astype(BF16)
        r1 = xt - x_hi.astype(F32)
        x_mid = r1.astype(BF16)
        x_lo = (r1 - x_mid.astype(F32)).astype(BF16)
        vals = _dot(x_hi, onehot) + _dot(x_mid, onehot) + _dot(x_lo, onehot)
        gate_ref[0, e] = jnp.sum(jnp.where(j_col == j_of_s, vals, 0.0), axis=0, keepdims=True)


def _route(aff_t, n_tokens, cap):
    b = aff_t.shape[0]
    g = n_tokens // LANES
    gp = -(-g // LANES) * LANES
    aff4 = aff_t.reshape(b, N_EXPERTS, g, LANES)
    if gp != g:
        aff4 = jnp.pad(aff4, ((0, 0), (0, 0), (0, gp - g), (0, 0)), constant_values=-1.0)
    return pl.pallas_call(
        functools.partial(_topk_kernel, gp=gp, cap=cap),
        grid=(b,),
        in_specs=[pl.BlockSpec((1, N_EXPERTS, gp, LANES), lambda bi: (bi, 0, 0, 0))],
        out_specs=[pl.BlockSpec((1, N_EXPERTS, 1, cap), lambda bi: (bi, 0, 0, 0)),
                   pl.BlockSpec((1, N_EXPERTS, 1, cap), lambda bi: (bi, 0, 0, 0))],
        out_shape=[jax.ShapeDtypeStruct((b, N_EXPERTS, 1, cap), jnp.int32),
                   jax.ShapeDtypeStruct((b, N_EXPERTS, 1, cap), F32)],
        compiler_params=_cparams(("parallel",)),
        name="route_topk",
    )(aff4)


def _ffn_kernel(idx_ref, xh_in, gate_ref, m_ref, w1_ref, w3_ref, w2_ref, xh_out,
                buf, wbf, sem_g, sem_o, *, cap, rows):
    b = pl.program_id(1)
    del xh_in
    d = D_MODEL
    n_chunks = cap // rows

    @pl.when(b == 0)
    def _():
        wbf[0] = w1_ref[0, 0].astype(BF16)
        wbf[1] = w3_ref[0, 0].astype(BF16)
        wbf[2] = w2_ref[0, 0].astype(BF16)

    def gather(c, slot):
        def body(r, _):
            tok = idx_ref[0, 0, c * rows + r]
            pltpu.make_async_copy(xh_out.at[b, pl.ds(tok, 1), :], buf.at[slot, pl.ds(r, 1), :],
                                  sem_g.at[slot]).start()
            return 0

        lax.fori_loop(0, rows, body, 0, unroll=8)

    def wait_gather(slot):
        pltpu.make_async_copy(xh_out.at[b, pl.ds(0, rows), :], buf.at[slot], sem_g.at[slot]).wait()

    def scatter(c, slot):
        def body(r, _):
            tok = idx_ref[0, 0, c * rows + r]
            pltpu.make_async_copy(buf.at[slot, pl.ds(r, 1), pl.ds(0, d)],
                                  xh_out.at[b, pl.ds(tok, 1), pl.ds(0, d)], sem_o.at[slot]).start()
            return 0

        lax.fori_loop(0, rows, body, 0, unroll=8)

    def wait_scatter(slot):
        pltpu.make_async_copy(buf.at[slot, :, pl.ds(0, d)], xh_out.at[b, pl.ds(0, rows), pl.ds(0, d)],
                              sem_o.at[slot]).wait()

    gather(0, 0)
    for c in range(n_chunks):
        slot = c % 2
        if c + 1 < n_chunks:
            if c >= 1:
                wait_scatter(1 - slot)
            gather(c + 1, 1 - slot)
        wait_gather(slot)
        xb = buf[slot, :, d:2 * d].astype(BF16)
        hid = _silu(_dot(xb, wbf[0])) * _dot(xb, wbf[1])
        y = _dot(hid.astype(BF16), wbf[2])
        y = y * gate_ref[0, 0, c * rows:(c + 1) * rows, :] * m_ref[0]
        buf[slot, :, 0:d] = buf[slot, :, 0:d] + y
        scatter(c, slot)
    if n_chunks >= 2:
        wait_scatter(n_chunks % 2)
    wait_scatter((n_chunks - 1) % 2)


def _expert_ffn(idx, gate, xh, mgate, w1, w3, w2, layer):
    b, n, d2 = xh.shape
    d = D_MODEL
    cap = idx.shape[-1]
    rows = min(cap, 256)
    wspec = lambda rows_, cols_: pl.BlockSpec((1, 1, rows_, cols_), lambda e, bi: (layer, e, 0, 0))
    return pl.pallas_call(
        functools.partial(_ffn_kernel, cap=cap, rows=rows),
        grid=(N_EXPERTS, b),
        in_specs=[pl.BlockSpec((1, 1, cap), lambda e, bi: (bi * N_EXPERTS + e, 0, 0), memory_space=pltpu.SMEM),
                  pl.BlockSpec(memory_space=pl.ANY),
                  pl.BlockSpec((1, 1, cap, 1), lambda e, bi: (bi, e, 0, 0)),
                  pl.BlockSpec((1, 1, d), lambda e, bi: (bi, 0, 0)),
                  wspec(d, D_EXPERT), wspec(d, D_EXPERT), wspec(D_EXPERT, d)],
        out_specs=pl.BlockSpec(memory_space=pl.ANY),
        out_shape=jax.ShapeDtypeStruct((b, n, d2), F32),
        scratch_shapes=[pltpu.VMEM((2, rows, d2), F32), pltpu.VMEM((3, d, D_EXPERT), BF16),
                        pltpu.SemaphoreType.DMA((2,)), pltpu.SemaphoreType.DMA((2,))],
        input_output_aliases={1: 0},
        compiler_params=pltpu.CompilerParams(dimension_semantics=("arbitrary", "arbitrary"),
                                             vmem_limit_bytes=VMEM_LIMIT_BYTES),
        name="expert_ffn",
    )(idx, xh, gate, mgate, w1, w3, w2)


def _final_norm_kernel(x_ref, w_ref, o_ref):
    x = x_ref[0]
    o_ref[0] = x * lax.rsqrt(jnp.mean(x * x, axis=-1, keepdims=True) + EPS) * w_ref[...]


def _final_norm(x, w, tm):
    b, n, _ = x.shape
    d = D_MODEL
    return pl.pallas_call(
        _final_norm_kernel,
        grid=(b, n // tm),
        in_specs=[pl.BlockSpec((1, tm, d), lambda bi, i: (bi, i, 0)), pl.BlockSpec((1, d), lambda bi, i: (0, 0))],
        out_specs=pl.BlockSpec((1, tm, d), lambda bi, i: (bi, i, 0)),
        out_shape=jax.ShapeDtypeStruct((b, n, d), F32),
        compiler_params=_cparams(("parallel", "parallel")),
        name="final_norm",
    )(x, w)


def _rope_tables(n):
    half = HEAD_DIM // 2
    n_freq = half // 2
    inv = ROPE_THETA ** (-jnp.arange(n_freq, dtype=F32) * 2.0 / half)
    rows = n // GRID_W
    r = jnp.repeat(jnp.arange(rows, dtype=F32), GRID_W)
    col = jnp.tile(jnp.arange(GRID_W, dtype=F32), rows)
    ang = jnp.stack([r[:, None] * inv, col[:, None] * inv], axis=1)
    cos = jnp.cos(ang)[:, :, None, :]
    sin = jnp.sin(ang)[:, :, None, :]
    cos_h = jnp.broadcast_to(cos, (n, 2, 2, n_freq)).reshape(n, HEAD_DIM)
    sin_h = (sin * jnp.array([-1.0, 1.0], F32)[None, None, :, None]).reshape(n, HEAD_DIM)
    return jnp.tile(cos_h, (1, 4)), jnp.tile(sin_h, (1, 4))


def _block_diag(w):
    nb, bi, bj = w.shape
    eye = jnp.eye(nb, dtype=w.dtype)
    return (eye[:, None, :, None] * w[:, :, None, :]).reshape(nb * bi, nb * bj)


def _tile_for(n, pref):
    t = pref
    while n % t:
        t //= 2
    return t


def kernel(x, c, ctx, c_ctx, mod_w, mod_b, norm1_w, w_in, a_conv_w, a_conv_b, a_gate_a_w, a_gate_a_b, a_gate_x_w,
           a_gate_x_b, a_lambda, b_lb_logits, b_onorm_w, c_sink, d_qnorm_w, d_knorm_w, w_out, norm2_w, router_w,
           exp_w1, exp_w3, exp_w2, final_norm_w):
    bsz, n, d = x.shape
    lc = ctx.shape[1]
    depth = mod_w.shape[0]
    cos, sin = _rope_tables(n)
    cos_c = jnp.ones((lc, 256), F32)
    sin_c = jnp.zeros((lc, 256), F32)
    lb_all = jnp.cumsum(jax.nn.softmax(b_lb_logits.astype(F32), axis=0), axis=0)
    lb_all = lb_all - lb_all[0]
    head_ids = jnp.arange(GROUP_W) // HEAD_DIM
    bd_ones = (head_ids[:, None] == head_ids[None, :]).astype(BF16)
    bd_mean = bd_ones * (1.0 / HEAD_DIM)

    s_rows = jnp.concatenate([c, c_ctx[None, :], jnp.zeros((SUBLANES - bsz - 1, d), F32)], axis=0)
    mods = _modulation(s_rows, mod_w, mod_b)

    tm = _tile_for(n, 512)
    tmc = _tile_for(lc, 512)
    ta = _tile_for(n, 512)
    tac = _tile_for(lc, 512)
    cap_l = max(1, CAP_FACTOR * n // N_EXPERTS)
    cap_c = max(1, CAP_FACTOR * lc // N_EXPERTS)

    xl, xc = x, ctx
    for l in range(depth):
        last = l == depth - 1
        m_l = mods[l, :bsz].reshape(bsz, N_MOD, d)
        m_c = jnp.broadcast_to(mods[l, bsz].reshape(1, N_MOD, d), (bsz, N_MOD, d))
        w_in_bf = w_in[l].astype(BF16)
        w_out_bf = w_out[l].astype(BF16)
        qw = jnp.tile(d_qnorm_w[l], 4)[None, :]
        kw = jnp.tile(d_knorm_w[l], 2)[None, :]
        nw1 = norm1_w[l][None, :]
        nw2 = norm2_w[l][None, :]
        conv_w = jnp.concatenate([a_conv_w[l], jnp.zeros((SUBLANES - CONV_W, GROUP_W), F32)], axis=0)
        conv_b = a_conv_b[l][None, :]
        gate_w = jnp.stack([jnp.concatenate([_block_diag(a_gate_a_w[l, dd]), _block_diag(a_gate_x_w[l, dd])], axis=1)
                            for dd in range(2)]).astype(BF16)
        gate_b = jnp.concatenate([a_gate_a_b[l], a_gate_x_b[l]], axis=1)[:, None, :]
        lam = a_lambda[l][:, None, :]
        lb = lb_all[l][None, :]
        onorm = jnp.tile(b_onorm_w[l], B_HEADS)[None, :]
        router_t = jnp.transpose(router_w[l])
        sink = c_sink[l].astype(F32)

        za_c, zb_c, qc_c, qd_c = _inproj(xc, nw1, m_c[:, 0:1], m_c[:, 1:2], w_in_bf, cos_c, sin_c, qw, kw, bd_mean, tmc)
        za_l, zb_l, qc_l, qd_l = _inproj(xl, nw1, m_l[:, 0:1], m_l[:, 1:2], w_in_bf, cos, sin, qw, kw, bd_mean, tm)

        h0 = jnp.zeros((bsz, SUBLANES, GROUP_W), F32)
        hf_c, hb_c, hlast = _lru(za_c, h0, conv_w, conv_b, gate_w, gate_b, lam, tac)
        hf_l, hb_l, _ = _lru(za_l, hlast, conv_w, conv_b, gate_w, gate_b, lam, ta)

        s0 = jnp.zeros((bsz, 2, GROUP_W, GROUP_W), F32)
        of_c, ob_c, slast = _hgrn(zb_c, s0, lb, bd_ones)
        of_l, ob_l, _ = _hgrn(zb_l, slast, lb, bd_ones)

        yc_l = _window_attention(qc_l, qc_c, sink, _tile_for(n, 256))
        yd_l = _dense_attention(qd_l, qd_c, _tile_for(n, 512), _tile_for(n, 4096), 256)

        def after_mixer(hf, hb, za, of, ob, zb, yc, yd, xin, mm, tile, cap):
            mrows = jnp.concatenate([mm[:, 2:5], jnp.zeros((bsz, SUBLANES - 3, d), F32)], axis=1)
            xh, aff_t = _outproj(hf, hb, za, of, ob, zb, yc, yd, xin, w_out_bf, mrows, nw2, onorm, bd_mean,
                                 router_t, tile)
            idx, gate = _route(aff_t, xin.shape[1], cap)
            idx = idx.reshape(bsz * N_EXPERTS, 1, cap)
            gate = gate.reshape(bsz, N_EXPERTS, cap, 1)
            return _expert_ffn(idx, gate, xh, mm[:, 5:6], exp_w1, exp_w3, exp_w2, l)

        xl = after_mixer(hf_l, hb_l, za_l, of_l, ob_l, zb_l, yc_l, yd_l, xl, m_l, tm, cap_l)
        if not last:
            yc_c = _ctx_attention(qc_c, sink, True)
            yd_c = _ctx_attention(qd_c, sink, False)
            xc = after_mixer(hf_c, hb_c, za_c, of_c, ob_c, zb_c, yc_c, yd_c, xc, m_c, tmc, cap_c)
    return _final_norm(xl, final_norm_w[None, :], tm)
```

```python
import functools
import math

import jax
import jax.numpy as jnp
import numpy as np
from jax import lax
from jax.experimental import pallas as pl
from jax.experimental.pallas import tpu as pltpu

D_MODEL = 1024
GRID_W = 64
GROUP_W = D_MODEL // 4
HEAD_DIM = 64
EPS = 1e-6
A_BLOCKS = 4
CONV_W = 4
LRU_C = 8.0
B_HEADS = 4
WINDOW = 128
ROPE_THETA = 10000.0
N_EXPERTS = 16
CAP_FACTOR = 2
D_EXPERT = 1024
D_IN = 2816
N_MOD = 6

LANES = 128
SUBLANES = 8
VMEM_LIMIT_BYTES = 56 * 1024 * 1024

NEG_BIG = -1e30
LOG2_E = math.log2(math.e)
F32 = jnp.float32
BF16 = jnp.bfloat16


def _cparams(sem):
    return pltpu.CompilerParams(dimension_semantics=sem, vmem_limit_bytes=VMEM_LIMIT_BYTES)


def _dot(a, b):
    return jnp.dot(a, b, preferred_element_type=F32)


def _dot_nt(a, b):
    return lax.dot_general(a, b, (((1,), (1,)), ((), ())), preferred_element_type=F32)


def _silu(x):
    return x * (1.0 / (1.0 + jnp.exp(-x)))


def _sigmoid(x):
    return 1.0 / (1.0 + jnp.exp(-x))


def _iota(shape, dim):
    return lax.broadcasted_iota(jnp.int32, shape, dim)


def _mod_kernel(s_ref, w_ref, b_ref, o_ref):
    s = _silu(s_ref[...])
    o_ref[0] = jnp.dot(s, w_ref[0], precision=lax.Precision.HIGHEST, preferred_element_type=F32) + b_ref[0]


def _modulation(s_rows, mod_w, mod_b):
    depth, d, n = mod_w.shape
    tn = 1536
    return pl.pallas_call(
        _mod_kernel,
        grid=(depth, n // tn),
        in_specs=[pl.BlockSpec((SUBLANES, d), lambda l, j: (0, 0)),
                  pl.BlockSpec((1, d, tn), lambda l, j: (l, 0, j)),
                  pl.BlockSpec((1, 1, tn), lambda l, j: (l, 0, j))],
        out_specs=pl.BlockSpec((1, SUBLANES, tn), lambda l, j: (l, 0, j)),
        out_shape=jax.ShapeDtypeStruct((depth, SUBLANES, n), F32),
        compiler_params=_cparams(("parallel", "parallel")),
        name="modulation",
    )(s_rows, mod_w, mod_b.reshape(depth, 1, n))


def _rope(x, cos, sin_signed):
    w = x.shape[-1]
    lane = _iota(x.shape, 1)
    partner = jnp.where((lane % 32) < 16, pltpu.roll(x, w - 16, 1), pltpu.roll(x, 16, 1))
    return x * cos + partner * sin_signed


def _head_rms(x, bd_mean, w):
    ms = _dot((x * x).astype(BF16), bd_mean)
    return x * lax.rsqrt(ms + EPS) * w


def _pack_qkv(o_ref, q, k, v):
    o_ref[0, :, 0:256] = q.astype(BF16)
    o_ref[0, :, 256:384] = k.astype(BF16)
    o_ref[0, :, 384:512] = pltpu.roll(k, 64, 1).astype(BF16)
    o_ref[0, :, 512:640] = v.astype(BF16)
    o_ref[0, :, 640:768] = pltpu.roll(v, 64, 1).astype(BF16)


def _inproj_kernel(x_ref, nw_ref, sh_ref, sc_ref, w_ref, cos_ref, sin_ref, qw_ref, kw_ref, bd_ref,
                   za_ref, zb_ref, qc_ref, qd_ref):
    x = x_ref[0]
    y = x * lax.rsqrt(jnp.mean(x * x, axis=-1, keepdims=True) + EPS) * nw_ref[...]
    h = y * (1.0 + sc_ref[0]) + sh_ref[0]
    z = _dot(h.astype(BF16), w_ref[...])
    za_ref[0] = z[:, 0:512]
    zb_ref[0] = z[:, 512:1792]
    cos = cos_ref[...]
    sin = sin_ref[...]
    scale = HEAD_DIM ** -0.5
    cq = _rope(z[:, 1792:2048], cos, sin) * scale
    ck = _rope(z[:, 2048:2176], cos[:, 0:128], sin[:, 0:128])
    _pack_qkv(qc_ref, cq, ck, z[:, 2176:2304])
    bd = bd_ref[...]
    dq = _rope(_head_rms(z[:, 2304:2560], bd, qw_ref[...]), cos, sin) * (scale * LOG2_E)
    dk = _rope(_head_rms(z[:, 2560:2688], bd[0:128, 0:128], kw_ref[...]), cos[:, 0:128], sin[:, 0:128])
    _pack_qkv(qd_ref, dq, dk, z[:, 2688:2816])


def _inproj(x, norm_w, shift, scale, w_in_bf, cos, sin, qw, kw, bd_mean, tm):
    b, n, _ = x.shape
    d = D_MODEL
    row = lambda bi, i: (bi, i, 0)
    const2 = lambda bi, i: (0, 0)
    return pl.pallas_call(
        _inproj_kernel,
        grid=(b, n // tm),
        in_specs=[pl.BlockSpec((1, tm, d), row),
                  pl.BlockSpec((1, d), const2),
                  pl.BlockSpec((1, 1, d), lambda bi, i: (bi, 0, 0)),
                  pl.BlockSpec((1, 1, d), lambda bi, i: (bi, 0, 0)),
                  pl.BlockSpec((d, D_IN), const2),
                  pl.BlockSpec((tm, 256), lambda bi, i: (i, 0)),
                  pl.BlockSpec((tm, 256), lambda bi, i: (i, 0)),
                  pl.BlockSpec((1, 256), const2),
                  pl.BlockSpec((1, 128), const2),
                  pl.BlockSpec((256, 256), const2)],
        out_specs=[pl.BlockSpec((1, tm, 512), row), pl.BlockSpec((1, tm, 1280), row),
                   pl.BlockSpec((1, tm, 768), row), pl.BlockSpec((1, tm, 768), row)],
        out_shape=[jax.ShapeDtypeStruct((b, n, 512), F32), jax.ShapeDtypeStruct((b, n, 1280), F32),
                   jax.ShapeDtypeStruct((b, n, 768), BF16), jax.ShapeDtypeStruct((b, n, 768), BF16)],
        compiler_params=_cparams(("parallel", "parallel")),
        name="inproj",
    )(x, norm_w, shift, scale, w_in_bf, cos, sin, qw, kw, bd_mean)


def _lru_kernel(fc_ref, fp_ref, fn_ref, bc_ref, bp_ref, bn_ref, h0_ref, cw_ref, cb_ref, gw_ref, gb_ref, lam_ref,
                hf_ref, hb_ref, hl_ref, carry_ref, *, t, nt):
    i = pl.program_id(1)

    @pl.when(i == 0)
    def _():
        carry_ref[...] = h0_ref[0]

    row = _iota((t, GROUP_W), 0)
    cw = cw_ref[...]

    def coeffs(cur_ref, prev_ref, next_ref, tile, d):
        prev = jnp.where(tile == 0, 0.0, prev_ref[0])
        nxt = jnp.where(tile == nt - 1, 0.0, next_ref[0])
        ext = jnp.concatenate([prev, cur_ref[0], nxt], axis=0)
        n_ext = t + 2 * SUBLANES
        xa = cb_ref[...] + cw[2:3] * ext[SUBLANES:SUBLANES + t]
        for j, off in ((0, 2), (1, 1), (3, -1)):
            xa = xa + cw[j:j + 1] * pltpu.roll(ext, off % n_ext, 0)[SUBLANES:SUBLANES + t]
        g = _dot(xa.astype(BF16), gw_ref[d]) + gb_ref[d]
        r = _sigmoid(g[:, 0:GROUP_W])
        gi = _sigmoid(g[:, GROUP_W:2 * GROUP_W])
        lam = lam_ref[d]
        softplus = jnp.maximum(-lam, 0.0) + jnp.log(1.0 + jnp.exp(-jnp.abs(lam)))
        log_a = (-LRU_C) * r * softplus
        a = jnp.exp(log_a)
        u = jnp.sqrt(1.0 - jnp.exp(2.0 * log_a)) * (gi * xa)
        return a, u

    a, u = coeffs(fc_ref, fp_ref, fn_ref, i, 0)
    s = 1
    while s < t:
        keep = row >= s
        a_sh = jnp.where(keep, pltpu.roll(a, s, 0), 1.0)
        u_sh = jnp.where(keep, pltpu.roll(u, s, 0), 0.0)
        u = a * u_sh + u
        a = a * a_sh
        s *= 2
    h = u + a * carry_ref[0:1]
    hf_ref[0] = h
    carry_ref[0:1] = hf_ref[0, t - 1:t, :]

    a, u = coeffs(bc_ref, bp_ref, bn_ref, nt - 1 - i, 1)
    s = 1
    while s < t:
        keep = row < t - s
        a_sh = jnp.where(keep, pltpu.roll(a, t - s, 0), 1.0)
        u_sh = jnp.where(keep, pltpu.roll(u, t - s, 0), 0.0)
        u = a * u_sh + u
        a = a * a_sh
        s *= 2
    h = u + a * carry_ref[1:2]
    hb_ref[0] = h
    carry_ref[1:2] = hb_ref[0, 0:1, :]

    @pl.when(i == nt - 1)
    def _():
        hl_ref[0] = carry_ref[...]


def _lru(za, h0, conv_w, conv_b, gate_w, gate_b, lam, t):
    b, n, _ = za.shape
    nt = n // t
    t8 = t // SUBLANES
    n8 = n // SUBLANES
    cur_f = lambda bi, i: (bi, i, 0)
    prev_f = lambda bi, i: (bi, jnp.maximum(i * t8 - 1, 0), 0)
    next_f = lambda bi, i: (bi, jnp.minimum((i + 1) * t8, n8 - 1), 0)
    cur_b = lambda bi, i: (bi, nt - 1 - i, 0)
    prev_b = lambda bi, i: (bi, jnp.maximum((nt - 1 - i) * t8 - 1, 0), 0)
    next_b = lambda bi, i: (bi, jnp.minimum((nt - i) * t8, n8 - 1), 0)
    c2 = lambda bi, i: (0, 0)
    c3 = lambda bi, i: (0, 0, 0)
    halo = (1, SUBLANES, GROUP_W)
    return pl.pallas_call(
        functools.partial(_lru_kernel, t=t, nt=nt),
        grid=(b, nt),
        in_specs=[pl.BlockSpec((1, t, GROUP_W), cur_f), pl.BlockSpec(halo, prev_f), pl.BlockSpec(halo, next_f),
                  pl.BlockSpec((1, t, GROUP_W), cur_b), pl.BlockSpec(halo, prev_b), pl.BlockSpec(halo, next_b),
                  pl.BlockSpec((1, SUBLANES, GROUP_W), lambda bi, i: (bi, 0, 0)),
                  pl.BlockSpec((SUBLANES, GROUP_W), c2), pl.BlockSpec((1, GROUP_W), c2),
                  pl.BlockSpec((2, GROUP_W, 2 * GROUP_W), c3), pl.BlockSpec((2, 1, 2 * GROUP_W), c3),
                  pl.BlockSpec((2, 1, GROUP_W), c3)],
        out_specs=[pl.BlockSpec((1, t, GROUP_W), cur_f), pl.BlockSpec((1, t, GROUP_W), cur_b),
                   pl.BlockSpec((1, SUBLANES, GROUP_W), lambda bi, i: (bi, 0, 0))],
        out_shape=[jax.ShapeDtypeStruct((b, n, GROUP_W), F32), jax.ShapeDtypeStruct((b, n, GROUP_W), F32),
                   jax.ShapeDtypeStruct((b, SUBLANES, GROUP_W), F32)],
        scratch_shapes=[pltpu.VMEM((SUBLANES, GROUP_W), F32)],
        compiler_params=_cparams(("parallel", "arbitrary")),
        name="rglru",
    )(za, za, za, za, za, za, h0, conv_w, conv_b, gate_w, gate_b, lam)


B_CHUNK = 128
B_SUB = 32


def _hgrn_chunk(q_raw, f_raw, v, lb, st_ref, d, b_ref, k_ref, v_ref, bd_ones, reverse):
    t = B_CHUNK
    n_sub = t // B_SUB
    q = _silu(q_raw) * (GROUP_W // B_HEADS) ** -0.5
    log_sig = jnp.minimum(f_raw, 0.0) - jnp.log(1.0 + jnp.exp(-jnp.abs(f_raw)))
    log_lb = jnp.log(lb)
    y = jnp.log(1.0 - lb) + log_sig
    log_f = jnp.maximum(log_lb, y) + jnp.log(1.0 + jnp.exp(-jnp.abs(log_lb - y)))
    k = (1.0 - lb) * _sigmoid(-f_raw)
    row = _iota((t, GROUP_W), 0)
    b = log_f * LOG2_E
    s = 1
    while s < t:
        if reverse:
            b = b + jnp.where(row < t - s, pltpu.roll(b, t - s, 0), 0.0)
        else:
            b = b + jnp.where(row >= s, pltpu.roll(b, s, 0), 0.0)
        s *= 2
    b_ref[...] = b
    k_ref[...] = b - jnp.log2(k)
    v_ref[...] = v
    edge = 0 if reverse else t - 1
    b_edge = b_ref[edge:edge + 1, :]
    st = st_ref[d]
    lane = _iota((B_SUB, GROUP_W), 1)
    head_masks = [(lane >= 64 * hh) & (lane < 64 * (hh + 1)) for hh in range(B_HEADS)]
    trow = _iota((B_SUB, GROUP_W), 0)
    srow = _iota((t, GROUP_W), 0)

    o_inter = _dot_nt((q * jnp.exp2(b)).astype(BF16), st.astype(BF16))
    outs = []
    for i in range(n_sub):
        lo = i * B_SUB
        qb = q[lo:lo + B_SUB]
        bb = b[lo:lo + B_SUB]
        acc = o_inter[lo:lo + B_SUB]
        has_off = (i < n_sub - 1) if reverse else (i > 0)
        if has_off:
            ref_row = lo + B_SUB if reverse else lo - 1
            r_i = b_ref[ref_row:ref_row + 1, :]
            qh = qb * jnp.exp2(bb - r_i)
            key_ok = (srow >= lo + B_SUB) if reverse else (srow < lo)
            kh = jnp.where(key_ok, k * jnp.exp2(jnp.minimum(r_i - b, 0.0)), 0.0)
            qstack = jnp.concatenate([jnp.where(m, qh, 0.0) for m in head_masks], axis=0).astype(BF16)
            att = _dot_nt(qstack, kh.astype(BF16))
            res = _dot(att.astype(BF16), v.astype(BF16))
            for hh in range(B_HEADS):
                acc = acc + jnp.where(head_masks[hh], res[hh * B_SUB:(hh + 1) * B_SUB], 0.0)
        pieces = []
        for sl in range(B_SUB):
            level = k_ref[lo + sl:lo + sl + 1, :]
            ok = (trow <= sl) if reverse else (trow >= sl)
            e = jnp.exp2(jnp.where(ok, bb - level, NEG_BIG)) * qb
            pieces.append(e.astype(BF16))
        g = _dot(jnp.concatenate(pieces, axis=0), bd_ones)
        for sl in range(B_SUB):
            acc = acc + g[sl * B_SUB:(sl + 1) * B_SUB] * v_ref[lo + sl:lo + sl + 1, :]
        outs.append(acc)
    kt = k * jnp.exp2(b_edge - b)
    upd = _dot(jnp.transpose(v).astype(BF16), kt.astype(BF16))
    vrow = _iota((GROUP_W, GROUP_W), 0) // 64
    kcol = _iota((GROUP_W, GROUP_W), 1) // 64
    st_ref[d] = st * jnp.exp2(b_edge) + jnp.where(vrow == kcol, upd, 0.0)
    return jnp.concatenate(outs, axis=0)


def _hgrn_kernel(qf_ref, ff_ref, vf_ref, qb_ref, fb_ref, vb_ref, s0_ref, lb_ref, bd_ref,
                 of_ref, ob_ref, sl_ref, st_ref, b_ref, k_ref, v_ref, *, nt):
    i = pl.program_id(1)

    @pl.when(i == 0)
    def _():
        st_ref[...] = s0_ref[0]

    lb = lb_ref[...]
    bd = bd_ref[...]
    of_ref[0] = _hgrn_chunk(qf_ref[0], ff_ref[0], vf_ref[0], lb, st_ref, 0, b_ref, k_ref, v_ref, bd, False)
    ob_ref[0] = _hgrn_chunk(qb_ref[0], fb_ref[0], vb_ref[0], lb, st_ref, 1, b_ref, k_ref, v_ref, bd, True)

    @pl.when(i == nt - 1)
    def _():
        sl_ref[0] = st_ref[...]


def _hgrn(zb, s0, lb, bd_ones):
    b, n, _ = zb.shape
    t = B_CHUNK
    nt = n // t
    blk = (1, t, GROUP_W)
    fwd = lambda col: (lambda bi, i: (bi, i, col))
    bwd = lambda col: (lambda bi, i: (bi, nt - 1 - i, col))
    state_spec = pl.BlockSpec((1, 2, GROUP_W, GROUP_W), lambda bi, i: (bi, 0, 0, 0))
    return pl.pallas_call(
        functools.partial(_hgrn_kernel, nt=nt),
        grid=(b, nt),
        in_specs=[pl.BlockSpec(blk, fwd(0)), pl.BlockSpec(blk, fwd(1)), pl.BlockSpec(blk, fwd(3)),
                  pl.BlockSpec(blk, bwd(0)), pl.BlockSpec(blk, bwd(2)), pl.BlockSpec(blk, bwd(3)),
                  state_spec,
                  pl.BlockSpec((1, GROUP_W), lambda bi, i: (0, 0)),
                  pl.BlockSpec((GROUP_W, GROUP_W), lambda bi, i: (0, 0))],
        out_specs=[pl.BlockSpec(blk, fwd(0)), pl.BlockSpec(blk, bwd(0)), state_spec],
        out_shape=[jax.ShapeDtypeStruct((b, n, GROUP_W), F32), jax.ShapeDtypeStruct((b, n, GROUP_W), F32),
                   jax.ShapeDtypeStruct((b, 2, GROUP_W, GROUP_W), F32)],
        scratch_shapes=[pltpu.VMEM((2, GROUP_W, GROUP_W), F32), pltpu.VMEM((t, GROUP_W), F32),
                        pltpu.VMEM((t, GROUP_W), F32), pltpu.VMEM((t, GROUP_W), F32)],
        compiler_params=_cparams(("parallel", "arbitrary")),
        name="hgrn2",
    )(zb, zb, zb, zb, zb, zb, s0, lb, bd_ones)


def _stack_q(q):
    tq = q.shape[0]
    lane = _iota((tq, LANES), 1)
    lo = lane < 64
    q0 = q[:, 0:128]
    q1 = q[:, 128:256]
    zero = jnp.zeros_like(q0)
    s1 = jnp.concatenate([jnp.where(lo, q0, zero), jnp.where(lo, zero, q1)], axis=0)
    s2 = jnp.concatenate([jnp.where(lo, zero, q0), jnp.where(lo, q1, zero)], axis=0)
    return s1, s2


def _unstack_out(r1, r2, tq):
    lane = _iota((tq, LANES), 1)
    lo = lane < 64
    o0 = jnp.where(lo, r1[0:tq], r2[0:tq])
    o1 = jnp.where(lo, r2[tq:2 * tq], r1[tq:2 * tq])
    return jnp.concatenate([o0, o1], axis=1)


def _sink_cols(sink_ref, tq):
    row = _iota((2 * tq, 1), 0)
    c1 = jnp.where(row < tq, sink_ref[0], sink_ref[3])
    c2 = jnp.where(row < tq, sink_ref[1], sink_ref[2])
    return c1, c2


def _window_kernel(sink_ref, q_ref, kp_ref, kc_ref, kn_ref, vp_ref, vc_ref, vn_ref, kx_ref, vx_ref, o_ref,
                   *, tq, n):
    i = pl.program_id(1)
    s1, s2 = _stack_q(q_ref[0])
    kband = jnp.concatenate([kp_ref[0], kc_ref[0], kn_ref[0]], axis=0)
    vband = jnp.concatenate([vp_ref[0], vc_ref[0], vn_ref[0]], axis=0)
    nband = tq + 2 * WINDOW
    qpos = i * tq + (_iota((2 * tq, nband), 0) % tq)
    kpos = i * tq - WINDOW + _iota((2 * tq, nband), 1)
    ok = (kpos >= 0) & (kpos < n) & (jnp.abs(qpos - kpos) <= WINDOW)
    sinks = _sink_cols(sink_ref, tq)
    res = []
    for qs, half, sink in ((s1, 0, sinks[0]), (s2, 1, sinks[1])):
        kx = kx_ref[0][:, half * 128:(half + 1) * 128]
        vx = vx_ref[0][:, half * 128:(half + 1) * 128]
        sc = _dot_nt(qs, kx)
        sw = jnp.where(ok, _dot_nt(qs, kband[:, half * 128:(half + 1) * 128]), NEG_BIG)
        m = jnp.maximum(jnp.maximum(jnp.max(sc, axis=-1, keepdims=True), jnp.max(sw, axis=-1, keepdims=True)), sink)
        pc = jnp.exp(sc - m)
        pw = jnp.exp(sw - m)
        den = jnp.sum(pc, axis=-1, keepdims=True) + jnp.sum(pw, axis=-1, keepdims=True) + jnp.exp(sink - m)
        num = _dot(pc.astype(BF16), vx) + _dot(pw.astype(BF16), vband[:, half * 128:(half + 1) * 128])
        res.append(num / den)
    o_ref[0] = _unstack_out(res[0], res[1], tq).astype(BF16)


def _window_attention(qkv, qkv_ctx, sink, tq):
    b, n, _ = qkv.shape
    lc = qkv_ctx.shape[1]
    w = WINDOW
    r = tq // w
    nw = n // w
    prev = lambda col: (lambda bi, i, s: (bi, jnp.maximum(i * r - 1, 0), col))
    cur = lambda col: (lambda bi, i, s: (bi, i, col))
    nxt = lambda col: (lambda bi, i, s: (bi, jnp.minimum((i + 1) * r, nw - 1), col))
    ctx = lambda col: (lambda bi, i, s: (bi, 0, col))
    grid_spec = pltpu.PrefetchScalarGridSpec(
        num_scalar_prefetch=1,
        grid=(b, n // tq),
        in_specs=[pl.BlockSpec((1, tq, 256), cur(0)),
                  pl.BlockSpec((1, w, 256), prev(1)), pl.BlockSpec((1, tq, 256), cur(1)),
                  pl.BlockSpec((1, w, 256), nxt(1)),
                  pl.BlockSpec((1, w, 256), prev(2)), pl.BlockSpec((1, tq, 256), cur(2)),
                  pl.BlockSpec((1, w, 256), nxt(2)),
                  pl.BlockSpec((1, lc, 256), ctx(1)), pl.BlockSpec((1, lc, 256), ctx(2))],
        out_specs=pl.BlockSpec((1, tq, 256), cur(0)),
    )
    return pl.pallas_call(
        functools.partial(_window_kernel, tq=tq, n=n),
        grid_spec=grid_spec,
        out_shape=jax.ShapeDtypeStruct((b, n, 256), BF16),
        compiler_params=_cparams(("parallel", "parallel")),
        name="window_attn",
    )(sink, qkv, qkv, qkv, qkv, qkv, qkv, qkv, qkv_ctx, qkv_ctx)


def _ctx_attn_kernel(sink_ref, q_ref, k_ref, v_ref, o_ref, *, tq, use_sink):
    s1, s2 = _stack_q(q_ref[0])
    sinks = _sink_cols(sink_ref, tq)
    res = []
    for qs, half, sink in ((s1, 0, sinks[0]), (s2, 1, sinks[1])):
        kx = k_ref[0][:, half * 128:(half + 1) * 128]
        vx = v_ref[0][:, half * 128:(half + 1) * 128]
        sc = _dot_nt(qs, kx)
        m = jnp.max(sc, axis=-1, keepdims=True)
        if use_sink:
            m = jnp.maximum(m, sink)
        p = jnp.exp(sc - m) if use_sink else jnp.exp2(sc - m)
        den = jnp.sum(p, axis=-1, keepdims=True)
        if use_sink:
            den = den + jnp.exp(sink - m)
        res.append(_dot(p.astype(BF16), vx) / den)
    o_ref[0] = _unstack_out(res[0], res[1], tq).astype(BF16)


def _ctx_attention(qkv_ctx, sink, use_sink):
    b, lc, _ = qkv_ctx.shape
    col = lambda c: (lambda bi, s: (bi, 0, c))
    grid_spec = pltpu.PrefetchScalarGridSpec(
        num_scalar_prefetch=1,
        grid=(b,),
        in_specs=[pl.BlockSpec((1, lc, 256), col(0)), pl.BlockSpec((1, lc, 256), col(1)),
                  pl.BlockSpec((1, lc, 256), col(2))],
        out_specs=pl.BlockSpec((1, lc, 256), col(0)),
    )
    return pl.pallas_call(
        functools.partial(_ctx_attn_kernel, tq=lc, use_sink=use_sink),
        grid_spec=grid_spec,
        out_shape=jax.ShapeDtypeStruct((b, lc, 256), BF16),
        compiler_params=_cparams(("parallel",)),
        name="ctx_attn",
    )(sink, qkv_ctx, qkv_ctx, qkv_ctx)


def _dense_kernel(q_ref, k_ref, v_ref, kx_ref, vx_ref, o_ref, qs_ref, m_ref, acc_ref, *, tq, tk, ck, rb, nk):
    ki = pl.program_id(2)
    rows = 2 * tq

    def update(half, kc, vc):
        n = kc.shape[0]
        vext = jnp.concatenate([vc, jnp.ones((n, LANES), BF16)], axis=1)
        for r in range(rows // rb):
            sl = slice(r * rb, (r + 1) * rb)
            s = _dot_nt(qs_ref[half, sl, :], kc)
            m_old = m_ref[half, sl, :]
            m_new = jnp.maximum(m_old, jnp.max(s, axis=-1, keepdims=True))
            alpha = jnp.exp2(m_old - m_new)
            p = jnp.exp2(s - jnp.tile(m_new, (1, n // LANES))).astype(BF16)
            m_ref[half, sl, :] = m_new
            acc_ref[half, sl, :] = acc_ref[half, sl, :] * jnp.tile(alpha, (1, 2)) + _dot(p, vext)

    @pl.when(ki == 0)
    def _():
        s1, s2 = _stack_q(q_ref[0])
        qs_ref[0] = s1
        qs_ref[1] = s2
        m_ref[...] = jnp.full(m_ref.shape, NEG_BIG, F32)
        acc_ref[...] = jnp.zeros(acc_ref.shape, F32)
        for half in range(2):
            update(half, kx_ref[0][:, half * 128:(half + 1) * 128], vx_ref[0][:, half * 128:(half + 1) * 128])

    for c in range(tk // ck):
        for half in range(2):
            update(half, k_ref[0, c * ck:(c + 1) * ck, half * 128:(half + 1) * 128],
                   v_ref[0, c * ck:(c + 1) * ck, half * 128:(half + 1) * 128])

    @pl.when(ki == nk - 1)
    def _():
        a1 = acc_ref[0]
        a2 = acc_ref[1]
        r1 = a1[:, 0:LANES] / a1[:, LANES:2 * LANES]
        r2 = a2[:, 0:LANES] / a2[:, LANES:2 * LANES]
        o_ref[0] = _unstack_out(r1, r2, tq).astype(BF16)


def _dense_attention(qkv, qkv_ctx, tq, tk, ck):
    b, n, _ = qkv.shape
    lc = qkv_ctx.shape[1]
    nk = n // tk
    rb = min(2 * tq, 256)
    return pl.pallas_call(
        functools.partial(_dense_kernel, tq=tq, tk=tk, ck=ck, rb=rb, nk=nk),
        grid=(b, n // tq, nk),
        in_specs=[pl.BlockSpec((1, tq, 256), lambda bi, qi, ki: (bi, qi, 0)),
                  pl.BlockSpec((1, tk, 256), lambda bi, qi, ki: (bi, ki, 1)),
                  pl.BlockSpec((1, tk, 256), lambda bi, qi, ki: (bi, ki, 2)),
                  pl.BlockSpec((1, lc, 256), lambda bi, qi, ki: (bi, 0, 1)),
                  pl.BlockSpec((1, lc, 256), lambda bi, qi, ki: (bi, 0, 2))],
        out_specs=pl.BlockSpec((1, tq, 256), lambda bi, qi, ki: (bi, qi, 0)),
        out_shape=jax.ShapeDtypeStruct((b, n, 256), BF16),
        scratch_shapes=[pltpu.VMEM((2, 2 * tq, LANES), BF16), pltpu.VMEM((2, 2 * tq, LANES), F32),
                        pltpu.VMEM((2, 2 * tq, 2 * LANES), F32)],
        compiler_params=_cparams(("parallel", "parallel", "arbitrary")),
        name="dense_attn",
    )(qkv, qkv, qkv, qkv_ctx, qkv_ctx)


def _gelu_tanh(x):
    return 0.5 * x * (1.0 + jnp.tanh(math.sqrt(2.0 / math.pi) * (x + 0.044715 * (x * x * x))))


def _outproj_kernel(hf_ref, hb_ref, ay_ref, of_ref, ob_ref, bg_ref, yc_ref, yd_ref, x_ref, w_ref, m_ref, nw_ref,
                    ow_ref, bd_ref, rw_ref, xh_ref, aff_ref):
    ya = (hf_ref[0] + hb_ref[0]) * _gelu_tanh(ay_ref[0])
    o = of_ref[0] + ob_ref[0]
    ms = _dot((o * o).astype(BF16), bd_ref[...])
    yb = o * lax.rsqrt(ms + EPS) * ow_ref[...] * _silu(bg_ref[0])
    y = jnp.concatenate([ya.astype(BF16), yb.astype(BF16), yc_ref[0], yd_ref[0]], axis=1)
    out = _dot(y, w_ref[...])
    m = m_ref[0]
    x = x_ref[0] + m[0:1] * out
    hn = x * lax.rsqrt(jnp.mean(x * x, axis=-1, keepdims=True) + EPS) * nw_ref[...]
    h2 = hn * (1.0 + m[2:3]) + m[1:2]
    xh_ref[0, :, 0:D_MODEL] = x
    xh_ref[0, :, D_MODEL:2 * D_MODEL] = h2
    logits = lax.dot_general(rw_ref[...], h2, (((1,), (1,)), ((), ())), precision=lax.Precision.HIGHEST,
                             preferred_element_type=F32)
    mx = jnp.max(logits, axis=0, keepdims=True)
    e = jnp.exp(logits - mx)
    aff_ref[0] = e / jnp.sum(e, axis=0, keepdims=True)


def _outproj(hf, hb, za, of, ob, zb, yc, yd, x, w_out_bf, mrows, norm2_w, onorm_w, bd_mean, router_t, tm):
    b, n, _ = x.shape
    d = D_MODEL
    row = lambda col: (lambda bi, i: (bi, i, col))
    c2 = lambda bi, i: (0, 0)
    g = (1, tm, GROUP_W)
    return pl.pallas_call(
        _outproj_kernel,
        grid=(b, n // tm),
        in_specs=[pl.BlockSpec(g, row(0)), pl.BlockSpec(g, row(0)), pl.BlockSpec(g, row(1)),
                  pl.BlockSpec(g, row(0)), pl.BlockSpec(g, row(0)), pl.BlockSpec(g, row(4)),
                  pl.BlockSpec(g, row(0)), pl.BlockSpec(g, row(0)),
                  pl.BlockSpec((1, tm, d), row(0)),
                  pl.BlockSpec((d, d), c2),
                  pl.BlockSpec((1, SUBLANES, d), lambda bi, i: (bi, 0, 0)),
                  pl.BlockSpec((1, d), c2), pl.BlockSpec((1, GROUP_W), c2),
                  pl.BlockSpec((GROUP_W, GROUP_W), c2), pl.BlockSpec((N_EXPERTS, d), c2)],
        out_specs=[pl.BlockSpec((1, tm, 2 * d), row(0)),
                   pl.BlockSpec((1, N_EXPERTS, tm), lambda bi, i: (bi, 0, i))],
        out_shape=[jax.ShapeDtypeStruct((b, n, 2 * d), F32),
                   jax.ShapeDtypeStruct((b, N_EXPERTS, n), F32)],
        compiler_params=_cparams(("parallel", "parallel")),
        name="outproj",
    )(hf, hb, za, of, ob, zb, yc, yd, x, w_out_bf, mrows, norm2_w, onorm_w, bd_mean, router_t)


def _topk_kernel(aff_ref, idx_ref, gate_ref, *, gp, cap):
    x_all = aff_ref[0]
    xb_all = pltpu.bitcast(x_all, jnp.int32)
    kf = float(cap)

    lo = jnp.zeros((N_EXPERTS, 1, 1), jnp.int32)
    hi = jnp.full((N_EXPERTS, 1, 1), 0x7F800000, jnp.int32)

    def bisect(_, carry):
        lo, hi = carry
        mid = lo + lax.shift_right_logical(hi - lo, 1)
        cnt = jnp.sum(jnp.sum(jnp.where(xb_all >= mid, 1.0, 0.0), axis=2, keepdims=True), axis=1, keepdims=True)
        ge = cnt >= kf
        return jnp.where(ge, mid, lo), jnp.where(ge, hi, mid)

    lo, hi = lax.fori_loop(0, 31, bisect, (lo, hi))

    upper = jnp.where(_iota((LANES, LANES), 0) <= _iota((LANES, LANES), 1), 1.0, 0.0).astype(BF16)
    strict = jnp.where(_iota((gp, gp), 1) < _iota((gp, gp), 0), 1.0, 0.0).astype(BF16)
    g_col = _iota((gp, 1), 0).astype(F32)
    j_col = _iota((LANES, 1), 0).astype(F32)
    slot = _iota((1, cap), 1).astype(F32)

    def prefix(mask_f):
        within = _dot(mask_f.astype(BF16), upper)
        before = jnp.sum(_dot(strict, mask_f.astype(BF16)), axis=1, keepdims=True)
        return within, before

    for e in range(N_EXPERTS):
        x = x_all[e]
        xb = xb_all[e]
        thr = lo[e]
        gt = jnp.where(xb > thr, 1.0, 0.0)
        eq = jnp.where(xb == thr, 1.0, 0.0)
        need = kf - jnp.sum(jnp.sum(gt, axis=1, keepdims=True), axis=0, keepdims=True)
        eq_within, eq_before = prefix(eq)
        tie_rank = eq_before + eq_within - eq
        sel = jnp.maximum(gt, jnp.where(tie_rank < need, eq, 0.0))
        within, before = prefix(sel)
        incl = before + within[:, LANES - 1:LANES]
        g_of_s = jnp.sum(jnp.where(incl <= slot, 1.0, 0.0), axis=0, keepdims=True)
        base = jnp.max(jnp.where(before <= slot, before, 0.0), axis=0, keepdims=True)
        rank = slot - base
        onehot = jnp.where(g_col == g_of_s, 1.0, 0.0).astype(BF16)
        counts = _dot(jnp.transpose(within).astype(BF16), onehot)
        j_of_s = jnp.sum(jnp.where(counts <= rank, 1.0, 0.0), axis=0, keepdims=True)
        idx_ref[0, e] = (g_of_s * float(LANES) + j_of_s).astype(jnp.int32)
        xt = jnp.transpose(x)
        x_hi = xt.astype(BF16)
        r1 = xt - x_hi.astype(F32)
        x_mid = r1.astype(BF16)
        x_lo = (r1 - x_mid.astype(F32)).astype(BF16)
        vals = _dot(x_hi, onehot) + _dot(x_mid, onehot) + _dot(x_lo, onehot)
        gate_ref[0, e] = jnp.sum(jnp.where(j_col == j_of_s, vals, 0.0), axis=0, keepdims=True)


def _route(aff_t, n_tokens, cap):
    b = aff_t.shape[0]
    g = n_tokens // LANES
    gp = -(-g // LANES) * LANES
    aff4 = aff_t.reshape(b, N_EXPERTS, g, LANES)
    if gp != g:
        aff4 = jnp.pad(aff4, ((0, 0), (0, 0), (0, gp - g), (0, 0)), constant_values=-1.0)
    return pl.pallas_call(
        functools.partial(_topk_kernel, gp=gp, cap=cap),
        grid=(b,),
        in_specs=[pl.BlockSpec((1, N_EXPERTS, gp, LANES), lambda bi: (bi, 0, 0, 0))],
        out_specs=[pl.BlockSpec((1, N_EXPERTS, 1, cap), lambda bi: (bi, 0, 0, 0)),
                   pl.BlockSpec((1, N_EXPERTS, 1, cap), lambda bi: (bi, 0, 0, 0))],
        out_shape=[jax.ShapeDtypeStruct((b, N_EXPERTS, 1, cap), jnp.int32),
                   jax.ShapeDtypeStruct((b, N_EXPERTS, 1, cap), F32)],
        compiler_params=_cparams(("parallel",)),
        name="route_topk",
    )(aff4)


def _ffn_kernel(idx_ref, xh_in, gate_ref, m_ref, w1_ref, w3_ref, w2_ref, xh_out,
                buf, wbf, sem_g, sem_o, *, cap, rows):
    b = pl.program_id(1)
    del xh_in
    d = D_MODEL
    n_chunks = cap // rows

    @pl.when(b == 0)
    def _():
        wbf[0] = w1_ref[0, 0].astype(BF16)
        wbf[1] = w3_ref[0, 0].astype(BF16)
        wbf[2] = w2_ref[0, 0].astype(BF16)

    def gather(c, slot):
        def body(r, _):
            tok = idx_ref[0, 0, c * rows + r]
            pltpu.make_async_copy(xh_out.at[b, pl.ds(tok, 1), :], buf.at[slot, pl.ds(r, 1), :],
                                  sem_g.at[slot]).start()
            return 0

        lax.fori_loop(0, rows, body, 0, unroll=8)

    def wait_gather(slot):
        pltpu.make_async_copy(xh_out.at[b, pl.ds(0, rows), :], buf.at[slot], sem_g.at[slot]).wait()

    def scatter(c, slot):
        def body(r, _):
            tok = idx_ref[0, 0, c * rows + r]
            pltpu.make_async_copy(buf.at[slot, pl.ds(r, 1), pl.ds(0, d)],
                                  xh_out.at[b, pl.ds(tok, 1), pl.ds(0, d)], sem_o.at[slot]).start()
            return 0

        lax.fori_loop(0, rows, body, 0, unroll=8)

    def wait_scatter(slot):
        pltpu.make_async_copy(buf.at[slot, :, pl.ds(0, d)], xh_out.at[b, pl.ds(0, rows), pl.ds(0, d)],
                              sem_o.at[slot]).wait()

    gather(0, 0)
    for c in range(n_chunks):
        slot = c % 2
        if c + 1 < n_chunks:
            if c >= 1:
                wait_scatter(1 - slot)
            gather(c + 1, 1 - slot)
        wait_gather(slot)
        xb = buf[slot, :, d:2 * d].astype(BF16)
        hid = _silu(_dot(xb, wbf[0])) * _dot(xb, wbf[1])
        y = _dot(hid.astype(BF16), wbf[2])
        y = y * gate_ref[0, 0, c * rows:(c + 1) * rows, :] * m_ref[0]
        buf[slot, :, 0:d] = buf[slot, :, 0:d] + y
        scatter(c, slot)
    if n_chunks >= 2:
        wait_scatter(n_chunks % 2)
    wait_scatter((n_chunks - 1) % 2)


def _expert_ffn(idx, gate, xh, mgate, w1, w3, w2, layer):
    b, n, d2 = xh.shape
    d = D_MODEL
    cap = idx.shape[-1]
    rows = min(cap, 256)
    wspec = lambda rows_, cols_: pl.BlockSpec((1, 1, rows_, cols_), lambda e, bi: (layer, e, 0, 0))
    return pl.pallas_call(
        functools.partial(_ffn_kernel, cap=cap, rows=rows),
        grid=(N_EXPERTS, b),
        in_specs=[pl.BlockSpec((1, 1, cap), lambda e, bi: (bi * N_EXPERTS + e, 0, 0), memory_space=pltpu.SMEM),
                  pl.BlockSpec(memory_space=pl.ANY),
                  pl.BlockSpec((1, 1, cap, 1), lambda e, bi: (bi, e, 0, 0)),
                  pl.BlockSpec((1, 1, d), lambda e, bi: (bi, 0, 0)),
                  wspec(d, D_EXPERT), wspec(d, D_EXPERT), wspec(D_EXPERT, d)],
        out_specs=pl.BlockSpec(memory_space=pl.ANY),
        out_shape=jax.ShapeDtypeStruct((b, n, d2), F32),
        scratch_shapes=[pltpu.VMEM((2, rows, d2), F32), pltpu.VMEM((3, d, D_EXPERT), BF16),
                        pltpu.SemaphoreType.DMA((2,)), pltpu.SemaphoreType.DMA((2,))],
        input_output_aliases={1: 0},
        compiler_params=pltpu.CompilerParams(dimension_semantics=("arbitrary", "arbitrary"),
                                             vmem_limit_bytes=VMEM_LIMIT_BYTES),
        name="expert_ffn",
    )(idx, xh, gate, mgate, w1, w3, w2)


def _final_norm_kernel(x_ref, w_ref, o_ref):
    x = x_ref[0]
    o_ref[0] = x * lax.rsqrt(jnp.mean(x * x, axis=-1, keepdims=True) + EPS) * w_ref[...]


def _final_norm(x, w, tm):
    b, n, _ = x.shape
    d = D_MODEL
    return pl.pallas_call(
        _final_norm_kernel,
        grid=(b, n // tm),
        in_specs=[pl.BlockSpec((1, tm, d), lambda bi, i: (bi, i, 0)), pl.BlockSpec((1, d), lambda bi, i: (0, 0))],
        out_specs=pl.BlockSpec((1, tm, d), lambda bi, i: (bi, i, 0)),
        out_shape=jax.ShapeDtypeStruct((b, n, d), F32),
        compiler_params=_cparams(("parallel", "parallel")),
        name="final_norm",
    )(x, w)


def _rope_tables(n):
    half = HEAD_DIM // 2
    n_freq = half // 2
    inv = ROPE_THETA ** (-jnp.arange(n_freq, dtype=F32) * 2.0 / half)
    rows = n // GRID_W
    r = jnp.repeat(jnp.arange(rows, dtype=F32), GRID_W)
    col = jnp.tile(jnp.arange(GRID_W, dtype=F32), rows)
    ang = jnp.stack([r[:, None] * inv, col[:, None] * inv], axis=1)
    cos = jnp.cos(ang)[:, :, None, :]
    sin = jnp.sin(ang)[:, :, None, :]
    cos_h = jnp.broadcast_to(cos, (n, 2, 2, n_freq)).reshape(n, HEAD_DIM)
    sin_h = (sin * jnp.array([-1.0, 1.0], F32)[None, None, :, None]).reshape(n, HEAD_DIM)
    return jnp.tile(cos_h, (1, 4)), jnp.tile(sin_h, (1, 4))


def _block_diag(w):
    nb, bi, bj = w.shape
    eye = jnp.eye(nb, dtype=w.dtype)
    return (eye[:, None, :, None] * w[:, :, None, :]).reshape(nb * bi, nb * bj)


def _tile_for(n, pref):
    t = pref
    while n % t:
        t //= 2
    return t


def kernel(x, c, ctx, c_ctx, mod_w, mod_b, norm1_w, w_in, a_conv_w, a_conv_b, a_gate_a_w, a_gate_a_b, a_gate_x_w,
           a_gate_x_b, a_lambda, b_lb_logits, b_onorm_w, c_sink, d_qnorm_w, d_knorm_w, w_out, norm2_w, router_w,
           exp_w1, exp_w3, exp_w2, final_norm_w):
    bsz, n, d = x.shape
    lc = ctx.shape[1]
    depth = mod_w.shape[0]
    cos, sin = _rope_tables(n)
    cos_c = jnp.ones((lc, 256), F32)
    sin_c = jnp.zeros((lc, 256), F32)
    lb_all = jnp.cumsum(jax.nn.softmax(b_lb_logits.astype(F32), axis=0), axis=0)
    lb_all = lb_all - lb_all[0]
    head_ids = jnp.arange(GROUP_W) // HEAD_DIM
    bd_ones = (head_ids[:, None] == head_ids[None, :]).astype(BF16)
    bd_mean = bd_ones * (1.0 / HEAD_DIM)

    s_rows = jnp.concatenate([c, c_ctx[None, :], jnp.zeros((SUBLANES - bsz - 1, d), F32)], axis=0)
    mods = _modulation(s_rows, mod_w, mod_b)

    tm = _tile_for(n, 512)
    tmc = _tile_for(lc, 512)
    ta = _tile_for(n, 512)
    tac = _tile_for(lc, 512)
    cap_l = max(1, CAP_FACTOR * n // N_EXPERTS)
    cap_c = max(1, CAP_FACTOR * lc // N_EXPERTS)

    xl, xc = x, ctx
    for l in range(depth):
        last = l == depth - 1
        m_l = mods[l, :bsz].reshape(bsz, N_MOD, d)
        m_c = jnp.broadcast_to(mods[l, bsz].reshape(1, N_MOD, d), (bsz, N_MOD, d))
        w_in_bf = w_in[l].astype(BF16)
        w_out_bf = w_out[l].astype(BF16)
        qw = jnp.tile(d_qnorm_w[l], 4)[None, :]
        kw = jnp.tile(d_knorm_w[l], 2)[None, :]
        nw1 = norm1_w[l][None, :]
        nw2 = norm2_w[l][None, :]
        conv_w = jnp.concatenate([a_conv_w[l], jnp.zeros((SUBLANES - CONV_W, GROUP_W), F32)], axis=0)
        conv_b = a_conv_b[l][None, :]
        gate_w = jnp.stack([jnp.concatenate([_block_diag(a_gate_a_w[l, dd]), _block_diag(a_gate_x_w[l, dd])], axis=1)
                            for dd in range(2)]).astype(BF16)
        gate_b = jnp.concatenate([a_gate_a_b[l], a_gate_x_b[l]], axis=1)[:, None, :]
        lam = a_lambda[l][:, None, :]
        lb = lb_all[l][None, :]
        onorm = jnp.tile(b_onorm_w[l], B_HEADS)[None, :]
        router_t = jnp.transpose(router_w[l])
        sink = c_sink[l].astype(F32)

        za_c, zb_c, qc_c, qd_c = _inproj(xc, nw1, m_c[:, 0:1], m_c[:, 1:2], w_in_bf, cos_c, sin_c, qw, kw, bd_mean, tmc)
        za_l, zb_l, qc_l, qd_l = _inproj(xl, nw1, m_l[:, 0:1], m_l[:, 1:2], w_in_bf, cos, sin, qw, kw, bd_mean, tm)

        h0 = jnp.zeros((bsz, SUBLANES, GROUP_W), F32)
        hf_c, hb_c, hlast = _lru(za_c, h0, conv_w, conv_b, gate_w, gate_b, lam, tac)
        hf_l, hb_l, _ = _lru(za_l, hlast, conv_w, conv_b, gate_w, gate_b, lam, ta)

        s0 = jnp.zeros((bsz, 2, GROUP_W, GROUP_W), F32)
        of_c, ob_c, slast = _hgrn(zb_c, s0, lb, bd_ones)
        of_l, ob_l, _ = _hgrn(zb_l, slast, lb, bd_ones)

        yc_l = _window_attention(qc_l, qc_c, sink, _tile_for(n, 256))
        yd_l = _dense_attention(qd_l, qd_c, _tile_for(n, 512), _tile_for(n, 8192), 256)

        def after_mixer(hf, hb, za, of, ob, zb, yc, yd, xin, mm, tile, cap):
            mrows = jnp.concatenate([mm[:, 2:5], jnp.zeros((bsz, SUBLANES - 3, d), F32)], axis=1)
            xh, aff_t = _outproj(hf, hb, za, of, ob, zb, yc, yd, xin, w_out_bf, mrows, nw2, onorm, bd_mean,
                                 router_t, tile)
            idx, gate = _route(aff_t, xin.shape[1], cap)
            idx = idx.reshape(bsz * N_EXPERTS, 1, cap)
            gate = gate.reshape(bsz, N_EXPERTS, cap, 1)
            return _expert_ffn(idx, gate, xh, mm[:, 5:6], exp_w1, exp_w3, exp_w2, l)

        xl = after_mixer(hf_l, hb_l, za_l, of_l, ob_l, zb_l, yc_l, yd_l, xl, m_l, tm, cap_l)
        if not last:
            yc_c = _ctx_attention(qc_c, sink, True)
            yd_c = _ctx_attention(qd_c, sink, False)
            xc = after_mixer(hf_c, hb_c, za_c, of_c, ob_c, zb_c, yc_c, yd_c, xc, m_c, tmc, cap_c)
    return _final_norm(xl, final_norm_w[None, :], tm)
```
